```python
import jax, jax.numpy as jnp
from jax import lax
import numpy as np

D_MODEL = 1024
BATCH = 4
SEQ = 4096
DEPTH = 4
DEC_BATCH = 32
DEC_SEQ = 64
PAST_LEN = 1024

CHUNK = 64
Q_BLOCK = 128
HEAD_DIM = 64
ROPE_THETA = 10000.0
NORM_EPS = 1e-6
A_HEADS = 4
IDX_HEADS = 4
IDX_DIM = 32
TOPK_MAX = 256
B_HEADS = 4
C_HEADS = 4
C_NOPE = 64
C_ROPE = 32
C_VDIM = 64
C_Q_RANK = 256
C_KV_RANK = 128
N_BRANCHES = 3
PEER_HEADS = 8
PEER_NKEYS = 128
PEER_EXPERTS = PEER_NKEYS * PEER_NKEYS
PEER_QDIM = 128
PEER_TOPK = 16
PEER_TOKEN_BLOCK = 128

IN_SPLITS = (
    A_HEADS * HEAD_DIM, A_HEADS * HEAD_DIM, A_HEADS * HEAD_DIM,
    IDX_HEADS * IDX_DIM, IDX_DIM, IDX_HEADS,
    B_HEADS * HEAD_DIM, B_HEADS * HEAD_DIM, B_HEADS * HEAD_DIM,
    B_HEADS,
    C_Q_RANK, C_KV_RANK, C_ROPE,
    N_BRANCHES * D_MODEL,
)
IN_COLS = int(sum(IN_SPLITS))
IN_OFFSETS = tuple(int(v) for v in np.cumsum(IN_SPLITS)[:-1])

kernel_name = 'hybrid_dsa_fox_mla_peer_stream_step'


def rmsnorm(x, g):
    xf = x.astype(jnp.float32)
    y = xf * lax.rsqrt(jnp.mean(xf * xf, axis=-1, keepdims=True) + NORM_EPS)
    return (y * g.astype(jnp.float32)).astype(x.dtype)


def rope(x, pos):
    half = x.shape[-1] // 2
    inv_freq = jnp.power(ROPE_THETA, -jnp.arange(half, dtype=jnp.float32) / half)
    ang = pos.astype(jnp.float32)[:, None] * inv_freq[None, :]
    cos = jnp.cos(ang)[None, :, None, :]
    sin = jnp.sin(ang)[None, :, None, :]
    xf = x.astype(jnp.float32)
    x1, x2 = xf[..., :half], xf[..., half:]
    return jnp.concatenate([x1 * cos - x2 * sin, x2 * cos + x1 * sin], axis=-1).astype(x.dtype)


def chunk_visible(q_pos, k_pos):
    return (k_pos // CHUNK)[None, :] <= (q_pos // CHUNK)[:, None]


def masked_softmax(logits, mask):
    return jax.nn.softmax(jnp.where(mask, logits.astype(jnp.float32), -jnp.inf), axis=-1)


def query_blocks(fn, q_args, q_pos):
    t = q_pos.shape[0]
    if t <= Q_BLOCK or t % Q_BLOCK:
        return fn(*q_args, q_pos)
    nb = t // Q_BLOCK

    def split(a):
        return jnp.moveaxis(a.reshape(a.shape[0], nb, Q_BLOCK, *a.shape[2:]), 1, 0)

    xs = tuple(split(a) for a in q_args) + (q_pos.reshape(nb, Q_BLOCK),)
    out = lax.map(lambda a: fn(*a[:-1], a[-1]), xs)
    out = jnp.moveaxis(out, 0, 1)
    return out.reshape(out.shape[0], t, *out.shape[3:])


def dsa_attend(q, iq, iw, q_pos, k, v, ik, k_pos, topk):
    rel = jax.nn.relu(jnp.einsum('bqhd,bsd->bqhs', iq.astype(jnp.float32), ik.astype(jnp.float32)))
    score = jnp.einsum('bqhs,bqh->bqs', rel, iw.astype(jnp.float32))
    score = jnp.where(chunk_visible(q_pos, k_pos)[None], score, -jnp.inf)
    _, idx = lax.top_k(score, topk)
    valid = (k_pos[idx] // CHUNK) <= (q_pos // CHUNK)[None, :, None]
    take = jax.vmap(lambda rows, ix: rows[ix])
    k_sel = take(k, idx)
    v_sel = take(v, idx)
    logits = jnp.einsum('bqhd,bqkhd->bhqk', q, k_sel) * HEAD_DIM ** -0.5
    p = masked_softmax(logits, valid[:, None])
    return jnp.einsum('bhqk,bqkhd->bqhd', p.astype(v.dtype), v_sel)


def fox_attend(q, cq, q_pos, k, v, ck, k_pos):
    logits = jnp.einsum('bqhd,bkhd->bhqk', q, k).astype(jnp.float32) * HEAD_DIM ** -0.5
    decay = jnp.moveaxis(cq, 1, 2)[..., :, None] - jnp.moveaxis(ck, 1, 2)[..., None, :]
    mask = (k_pos[None, :] <= q_pos[:, None])[None, None]
    p = masked_softmax(logits + decay, mask)
    return jnp.einsum('bhqk,bkhd->bqhd', p.astype(v.dtype), v)


def mla_attend(qn, qr, q_pos, kn, kr, v, k_pos):
    logits = (jnp.einsum('bqhd,bkhd->bhqk', qn, kn) + jnp.einsum('bqhd,bkd->bhqk', qr, kr)) * (C_NOPE + C_ROPE) ** -0.5
    p = masked_softmax(logits, chunk_visible(q_pos, k_pos)[None, None])
    return jnp.einsum('bhqk,bkhd->bqhd', p.astype(v.dtype), v)


def peer_ffn(h, w_q, sub_keys, u, v):
    bsz, t, d = h.shape
    n = bsz * t
    nb = -(-n // PEER_TOKEN_BLOCK)
    pad = nb * PEER_TOKEN_BLOCK - n
    flat = jnp.pad(h.reshape(n, d), ((0, pad), (0, 0))).reshape(nb, PEER_TOKEN_BLOCK, d)
    n_cand = PEER_TOPK * PEER_TOPK

    def block(hb):
        q = (hb @ w_q).reshape(PEER_TOKEN_BLOCK, PEER_HEADS, 2, PEER_QDIM // 2)
        s = jnp.einsum('thcd,hckd->thck', q.astype(jnp.float32), sub_keys.astype(jnp.float32))
        s1, i1 = lax.top_k(s[:, :, 0], PEER_TOPK)
        s2, i2 = lax.top_k(s[:, :, 1], PEER_TOPK)
        cand_s = (s1[..., :, None] + s2[..., None, :]).reshape(PEER_TOKEN_BLOCK, PEER_HEADS, n_cand)
        cand_i = (i1[..., :, None] * PEER_NKEYS + i2[..., None, :]).reshape(PEER_TOKEN_BLOCK, PEER_HEADS, n_cand)
        top_s, j = lax.top_k(cand_s, PEER_TOPK)
        e = jnp.take_along_axis(cand_i, j, axis=-1)
        g = jax.nn.softmax(top_s, axis=-1).astype(hb.dtype)
        act = jax.nn.gelu(jnp.einsum('td,thkd->thk', hb, u[e]))
        return jnp.einsum('thk,thkd->td', g * act, v[e])

    out = lax.map(block, flat).reshape(nb * PEER_TOKEN_BLOCK, d)[:n]
    return out.reshape(bsz, t, d)


def trunk_layer(x, pos, past, norm1_g, w_in, c_q_norm_g, c_kv_norm_g, w_uq, w_ukv, b_forget,
                w_up_a, w_up_b, w_up_c, w_out, norm2_g, peer_w_q, peer_keys, peer_u, peer_v):
    bsz, t = x.shape[0], x.shape[1]
    h = rmsnorm(x, norm1_g)
    (a_q, a_k, a_v, i_q, i_k, i_w, b_q, b_k, b_v, b_f, c_q, c_kv, c_kr, gates) = jnp.split(h @ w_in, IN_OFFSETS, axis=-1)

    def heads(z, n_heads):
        return z.reshape(bsz, t, n_heads, -1)

    a_q = rope(heads(a_q, A_HEADS), pos)
    a_k = rope(heads(a_k, A_HEADS), pos)
    a_v = heads(a_v, A_HEADS)
    i_q = rope(heads(i_q, IDX_HEADS), pos)
    i_k = rope(i_k[:, :, None, :], pos)[:, :, 0]
    i_w = i_w * (IDX_HEADS * IDX_DIM) ** -0.5
    b_q = heads(b_q, B_HEADS)
    b_k = heads(b_k, B_HEADS)
    b_v = heads(b_v, B_HEADS)
    logf = jax.nn.log_sigmoid(b_f.astype(jnp.float32) + b_forget.astype(jnp.float32)).astype(x.dtype)
    q_c = (rmsnorm(c_q, c_q_norm_g) @ w_uq).reshape(bsz, t, C_HEADS, C_NOPE + C_ROPE)
    q_nope = q_c[..., :C_NOPE]
    q_rope = rope(q_c[..., C_NOPE:], pos)
    ckv = rmsnorm(c_kv, c_kv_norm_g)
    krope = rope(c_kr[:, :, None, :], pos)[:, :, 0]

    new_rows = (a_k, a_v, i_k, b_k, b_v, logf, ckv, krope)
    if past is None:
        k_a, v_a, ki_a, k_b, v_b, lf_b, ckv_all, kr_all = new_rows
    else:
        k_a, v_a, ki_a, k_b, v_b, lf_b, ckv_all, kr_all = [
            jnp.concatenate([c.astype(nr.dtype), nr], axis=1) for c, nr in zip(past, new_rows)]
    n_keys = k_a.shape[1]
    k_pos = jnp.arange(n_keys, dtype=jnp.int32)
    topk = min(TOPK_MAX, n_keys // 4)

    o_a = query_blocks(lambda q, iq, iw, qp: dsa_attend(q, iq, iw, qp, k_a, v_a, ki_a, k_pos, topk),
                       (a_q, i_q, i_w), pos)
    cum_f = jnp.cumsum(lf_b.astype(jnp.float32), axis=1)
    o_b = query_blocks(lambda q, cq, qp: fox_attend(q, cq, qp, k_b, v_b, cum_f, k_pos),
                       (b_q, cum_f[:, n_keys - t:]), pos)
    kv_up = (ckv_all @ w_ukv).reshape(bsz, n_keys, C_HEADS, C_NOPE + C_VDIM)
    k_nope, v_c = kv_up[..., :C_NOPE], kv_up[..., C_NOPE:]
    o_c = query_blocks(lambda qn, qr, qp: mla_attend(qn, qr, qp, k_nope, kr_all, v_c, k_pos),
                       (q_nope, q_rope), pos)

    g = jax.nn.sigmoid(gates.astype(jnp.float32)).astype(x.dtype).reshape(bsz, t, N_BRANCHES, D_MODEL)
    merged = (g[:, :, 0] * (o_a.reshape(bsz, t, -1) @ w_up_a)
              + g[:, :, 1] * (o_b.reshape(bsz, t, -1) @ w_up_b)
              + g[:, :, 2] * (o_c.reshape(bsz, t, -1) @ w_up_c))
    x = x + merged @ w_out
    x = x + peer_ffn(rmsnorm(x, norm2_g), peer_w_q, peer_keys, peer_u, peer_v)
    return x, new_rows


def setup_inputs(seed: int = 0) -> dict:
    key = jax.random.key(seed)
    ks = iter(jax.random.split(key, 32))

    def nrm(shape, scale):
        return jax.random.normal(next(ks), shape, jnp.float32) * scale

    def gain(shape):
        return 1.0 + nrm(shape, 0.02)

    cb = (DEPTH, DEC_BATCH, PAST_LEN)
    return {
        'x_prompt': nrm((BATCH, SEQ, D_MODEL), 1.0),
        'x_sample': nrm((DEC_BATCH, DEC_SEQ, D_MODEL), 1.0),
        'cache_k_a': nrm(cb + (A_HEADS, HEAD_DIM), 1.0),
        'cache_v_a': nrm(cb + (A_HEADS, HEAD_DIM), 1.0),
        'cache_kidx_a': nrm(cb + (IDX_DIM,), 1.0),
        'cache_k_b': nrm(cb + (B_HEADS, HEAD_DIM), 1.0),
        'cache_v_b': nrm(cb + (B_HEADS, HEAD_DIM), 1.0),
        'cache_logf_b': jax.nn.log_sigmoid(3.0 + nrm(cb + (B_HEADS,), 0.5)),
        'cache_ckv_c': nrm(cb + (C_KV_RANK,), 1.0),
        'cache_krope_c': nrm(cb + (C_ROPE,), 1.0),
        'norm1_g': gain((DEPTH, D_MODEL)),
        'w_in': nrm((DEPTH, D_MODEL, IN_COLS), D_MODEL ** -0.5),
        'c_q_norm_g': gain((DEPTH, C_Q_RANK)),
        'c_kv_norm_g': gain((DEPTH, C_KV_RANK)),
        'w_uq': nrm((DEPTH, C_Q_RANK, C_HEADS * (C_NOPE + C_ROPE)), C_Q_RANK ** -0.5),
        'w_ukv': nrm((DEPTH, C_KV_RANK, C_HEADS * (C_NOPE + C_VDIM)), C_KV_RANK ** -0.5),
        'b_forget': 3.0 + nrm((DEPTH, B_HEADS), 0.5),
        'w_up_a': nrm((DEPTH, A_HEADS * HEAD_DIM, D_MODEL), (A_HEADS * HEAD_DIM) ** -0.5),
        'w_up_b': nrm((DEPTH, B_HEADS * HEAD_DIM, D_MODEL), (B_HEADS * HEAD_DIM) ** -0.5),
        'w_up_c': nrm((DEPTH, C_HEADS * C_VDIM, D_MODEL), (C_HEADS * C_VDIM) ** -0.5),
        'w_out': nrm((DEPTH, D_MODEL, D_MODEL), D_MODEL ** -0.5),
        'norm2_g': gain((DEPTH, D_MODEL)),
        'peer_w_q': nrm((DEPTH, D_MODEL, PEER_HEADS * PEER_QDIM), D_MODEL ** -0.5),
        'peer_keys': nrm((DEPTH, PEER_HEADS, 2, PEER_NKEYS, PEER_QDIM // 2), (PEER_QDIM // 2) ** -0.5),
        'peer_u': nrm((DEPTH, PEER_EXPERTS, D_MODEL), D_MODEL ** -0.5),
        'peer_v': nrm((DEPTH, PEER_EXPERTS, D_MODEL), (PEER_HEADS * PEER_TOPK) ** -0.5),
        'final_norm_g': gain((D_MODEL,)),
    }


def reference(x_prompt, x_sample, cache_k_a, cache_v_a, cache_kidx_a, cache_k_b, cache_v_b, cache_logf_b,
              cache_ckv_c, cache_krope_c, norm1_g, w_in, c_q_norm_g, c_kv_norm_g, w_uq, w_ukv, b_forget,
              w_up_a, w_up_b, w_up_c, w_out, norm2_g, peer_w_q, peer_keys, peer_u, peer_v, final_norm_g):
    past_len = cache_k_a.shape[2]
    pos_p = jnp.arange(x_prompt.shape[1], dtype=jnp.int32)
    pos_s = past_len + jnp.arange(x_sample.shape[1], dtype=jnp.int32)
    xp, xs = x_prompt, x_sample
    rows_p, rows_s = [], []
    for l in range(DEPTH):
        lw = (norm1_g[l], w_in[l], c_q_norm_g[l], c_kv_norm_g[l], w_uq[l], w_ukv[l], b_forget[l],
              w_up_a[l], w_up_b[l], w_up_c[l], w_out[l], norm2_g[l], peer_w_q[l], peer_keys[l],
              peer_u[l], peer_v[l])
        xp, rp = trunk_layer(xp, pos_p, None, *lw)
        past = (cache_k_a[l], cache_v_a[l], cache_kidx_a[l], cache_k_b[l], cache_v_b[l],
                cache_logf_b[l], cache_ckv_c[l], cache_krope_c[l])
        xs, rs = trunk_layer(xs, pos_s, past, *lw)
        rows_p.append(rp)
        rows_s.append(rs)
    y_prompt = rmsnorm(xp, final_norm_g)
    y_sample = rmsnorm(xs, final_norm_g)
    (new_k_a_p, new_v_a_p, new_kidx_a_p, new_k_b_p, new_v_b_p, new_logf_b_p,
     new_ckv_c_p, new_krope_c_p) = [jnp.stack(z) for z in zip(*rows_p)]
    (new_k_a_s, new_v_a_s, new_kidx_a_s, new_k_b_s, new_v_b_s, new_logf_b_s,
     new_ckv_c_s, new_krope_c_s) = [jnp.stack(z) for z in zip(*rows_s)]
    return (y_prompt, y_sample,
            new_k_a_p, new_v_a_p, new_kidx_a_p, new_k_b_p, new_v_b_p, new_logf_b_p, new_ckv_c_p, new_krope_c_p,
            new_k_a_s, new_v_a_s, new_kidx_a_s, new_k_b_s, new_v_b_s, new_logf_b_s, new_ckv_c_s, new_krope_c_s)
```

```python
import functools

import jax
import jax.numpy as jnp
import numpy as np
from jax import lax
from jax.experimental import pallas as pl
from jax.experimental.pallas import tpu as pltpu

F32 = jnp.float32
BF16 = jnp.bfloat16
I32 = jnp.int32

CHUNK = 64
HEAD_DIM = 64
ROPE_THETA = 10000.0
NORM_EPS = 1e-6
A_HEADS = 4
IDX_HEADS = 4
IDX_DIM = 32
TOPK_MAX = 256
B_HEADS = 4
C_HEADS = 4
C_NOPE = 64
C_ROPE = 32
C_VDIM = 64
C_Q_RANK = 256
C_KV_RANK = 128
N_BRANCHES = 3
PEER_HEADS = 8
PEER_QDIM = 128
PEER_TOPK = 16

HW = A_HEADS * HEAD_DIM
LANES = 128
VMEM_LIMIT = 56 * 1024 * 1024

SM_IK = 0
SM_KR = 32
SM_BF = 64
SM_IW = 72

MC_AQ, MC_AK, MC_AV = 0, 256, 512
MC_BQ, MC_BK, MC_BV = 768, 1024, 1280
MC_CQ = 1536
MC_CKV = 1792
MC_IQ = 1920
MC_SM = 2048
MAIN_COLS = 2176

RT_64 = 0
RT_IQ = 256
RT_SM = 384
RT_QC = 512
RT_COLS = 1024

NEG = -1e30
KEY_NEG_INF = -2139095041
INT_MIN = -2147483648


def _cparams(sem):
    return pltpu.CompilerParams(dimension_semantics=sem, vmem_limit_bytes=VMEM_LIMIT)


def _rmsnorm(x, g):
    ms = jnp.mean(x * x, axis=-1, keepdims=True)
    return x * lax.rsqrt(ms + NORM_EPS) * g


def _dot(a, b):
    return jnp.dot(a, b, preferred_element_type=F32)


def _dot_nt(a, b):
    return lax.dot_general(a, b, (((1,), (1,)), ((), ())), preferred_element_type=F32)


def _rope(v, c, s, half):
    w = v.shape[-1]
    lane = lax.broadcasted_iota(I32, v.shape, 1)
    first = (lane & (2 * half - 1)) < half
    vr = jnp.where(first, pltpu.roll(v, w - half, 1), pltpu.roll(v, half, 1))
    return v * c + vr * s


def _inproj_kernel(x_ref, g1_ref, wmain_ref, wuq_ref, gq_ref, gkv_ref, aux_ref, ct_ref, st_ref,
                   aq_ref, akf_ref, akb_ref, avf_ref, avb_ref,
                   bq_ref, bkf_ref, bkb_ref, bvf_ref, bvb_ref,
                   qc_ref, ckv_ref, iq_ref, sm_ref, lft_ref):
    x = x_ref[...]
    h = _rmsnorm(x, g1_ref[...]).astype(BF16)
    y = _dot(h, wmain_ref[...])
    ct = ct_ref[...]
    st = st_ref[...]

    c64 = ct[:, RT_64:RT_64 + HW]
    s64 = st[:, RT_64:RT_64 + HW]
    aq = _rope(y[:, MC_AQ:MC_AQ + HW], c64, s64, HEAD_DIM // 2)
    aq_ref[...] = (aq * (HEAD_DIM ** -0.5)).astype(BF16)
    ak = _rope(y[:, MC_AK:MC_AK + HW], c64, s64, HEAD_DIM // 2)
    akf_ref[...] = ak
    akb_ref[...] = ak.astype(BF16)
    av = y[:, MC_AV:MC_AV + HW]
    avf_ref[...] = av
    avb_ref[...] = av.astype(BF16)

    bq_ref[...] = (y[:, MC_BQ:MC_BQ + HW] * (HEAD_DIM ** -0.5)).astype(BF16)
    bk = y[:, MC_BK:MC_BK + HW]
    bkf_ref[...] = bk
    bkb_ref[...] = bk.astype(BF16)
    bv = y[:, MC_BV:MC_BV + HW]
    bvf_ref[...] = bv
    bvb_ref[...] = bv.astype(BF16)

    cq = _rmsnorm(y[:, MC_CQ:MC_CQ + C_Q_RANK], gq_ref[...]).astype(BF16)
    qc = _dot(cq, wuq_ref[...])
    qc = _rope(qc, ct[:, RT_QC:RT_QC + 512], st[:, RT_QC:RT_QC + 512], C_ROPE // 2)
    qc_ref[...] = (qc * ((C_NOPE + C_ROPE) ** -0.5)).astype(BF16)

    ckv_ref[...] = _rmsnorm(y[:, MC_CKV:MC_CKV + C_KV_RANK], gkv_ref[...])

    iq_ref[...] = _rope(y[:, MC_IQ:MC_IQ + LANES], ct[:, RT_IQ:RT_IQ + LANES],
                        st[:, RT_IQ:RT_IQ + LANES], IDX_DIM // 2)

    sm = _rope(y[:, MC_SM:MC_SM + LANES], ct[:, RT_SM:RT_SM + LANES],
               st[:, RT_SM:RT_SM + LANES], IDX_DIM // 2)
    lane = lax.broadcasted_iota(I32, sm.shape, 1)
    is_f = (lane >= SM_BF) & (lane < SM_BF + B_HEADS)
    z = sm + aux_ref[0:1, :]
    logsig = jnp.minimum(z, 0.0) - jnp.log1p(jnp.exp(-jnp.abs(z)))
    sm = jnp.where(is_f, logsig, sm * aux_ref[1:2, :])
    sm_ref[...] = sm
    lft_ref[...] = sm.T[SM_BF:SM_BF + 8, :]


def _inproj(x, g1, wmain, wuq, gq, gkv, aux, ctab, stab, *, tm, n_prompt, seq):
    n, d = x.shape
    nt = n // tm
    npt = n_prompt // tm
    tps = seq // tm

    def tmap(i):
        return (jnp.where(i < npt, i % tps, tps), 0)

    row = lambda w: pl.BlockSpec((tm, w), lambda i: (i, 0))
    full = lambda a: pl.BlockSpec(a.shape, lambda i: (0,) * a.ndim)
    out_shapes = [
        jax.ShapeDtypeStruct((n, HW), BF16),
        jax.ShapeDtypeStruct((n, HW), F32),
        jax.ShapeDtypeStruct((n, HW), BF16),
        jax.ShapeDtypeStruct((n, HW), F32),
        jax.ShapeDtypeStruct((n, HW), BF16),
        jax.ShapeDtypeStruct((n, HW), BF16),
        jax.ShapeDtypeStruct((n, HW), F32),
        jax.ShapeDtypeStruct((n, HW), BF16),
        jax.ShapeDtypeStruct((n, HW), F32),
        jax.ShapeDtypeStruct((n, HW), BF16),
        jax.ShapeDtypeStruct((n, 512), BF16),
        jax.ShapeDtypeStruct((n, LANES), F32),
        jax.ShapeDtypeStruct((n, LANES), F32),
        jax.ShapeDtypeStruct((n, LANES), F32),
        jax.ShapeDtypeStruct((8, n), F32),
    ]
    out_specs = [row(HW)] * 10 + [row(512), row(LANES), row(LANES), row(LANES),
                                  pl.BlockSpec((8, tm), lambda i: (0, i))]
    return pl.pallas_call(
        _inproj_kernel,
        grid=(nt,),
        in_specs=[row(d), full(g1), full(wmain), full(wuq), full(gq), full(gkv), full(aux),
                  pl.BlockSpec((tm, RT_COLS), tmap), pl.BlockSpec((tm, RT_COLS), tmap)],
        out_specs=out_specs,
        out_shape=out_shapes,
        compiler_params=_cparams(("parallel",)),
        name="inproj",
    )(x, g1, wmain, wuq, gq, gkv, aux, ctab, stab)


def _head_lane_mask(shape, h, width=HEAD_DIM):
    lane = lax.broadcasted_iota(I32, shape, 1)
    return (lane >= h * width) & (lane < (h + 1) * width)


def _split_heads_q(q):
    return [jnp.where(_head_lane_mask(q.shape, h), q, jnp.zeros_like(q)) for h in range(4)]


def _flash_init(m_ref, l_ref, acc_ref):
    m_ref[...] = jnp.full(m_ref.shape, NEG, F32)
    l_ref[...] = jnp.zeros(l_ref.shape, F32)
    acc_ref[...] = jnp.zeros(acc_ref.shape, F32)


def _flash_update(h, logits, sel, v_blk, m_ref, l_ref, acc_ref):
    lm = jnp.where(sel, logits, NEG)
    m_old = m_ref[h]
    m_new = jnp.maximum(m_old, jnp.max(lm, axis=1, keepdims=True))
    p = jnp.where(sel, jnp.exp(logits - m_new), 0.0)
    alpha = jnp.exp(m_old - m_new)
    l_ref[h] = alpha * l_ref[h] + jnp.sum(p, axis=1, keepdims=True)
    acc_ref[h] = alpha * acc_ref[h] + _dot(p.astype(BF16), v_blk)
    m_ref[h] = m_new


def _flash_finish(l_ref, acc_ref):
    out = None
    for h in range(4):
        o = acc_ref[h] / l_ref[h]
        o = jnp.where(_head_lane_mask(o.shape, h), o, 0.0)
        out = o if out is None else out + o
    return out.astype(BF16)


def _direct_attend(logits_list, sel, v_blk):
    out = None
    for h, logits in enumerate(logits_list):
        lm = jnp.where(sel, logits, NEG)
        m = jnp.max(lm, axis=1, keepdims=True)
        p = jnp.where(sel, jnp.exp(logits - m), 0.0)
        l = jnp.sum(p, axis=1, keepdims=True)
        o = _dot(p.astype(BF16), v_blk) / l
        o = jnp.where(_head_lane_mask(o.shape, h), o, 0.0)
        out = o if out is None else out + o
    return out.astype(BF16)


def _lane_cumsum(x):
    n = x.shape[1]
    lane = lax.broadcasted_iota(I32, x.shape, 1)
    k = 1
    while k < n:
        x = x + jnp.where(lane >= k, pltpu.roll(x, k, 1), 0.0)
        k *= 2
    return x


def _key_to_float(key):
    bits = key ^ ((key >> 31) & 0x7FFFFFFF)
    return jnp.where(key < KEY_NEG_INF, -jnp.inf, pltpu.bitcast(bits, F32))


def _kth_largest(count_ge, rows, k):
    zero = jnp.zeros((rows, 1), I32)
    c0 = count_ge(_key_to_float(zero))
    lo = jnp.where(c0 >= k, zero, jnp.full((rows, 1), INT_MIN, I32))

    def bit_body(i, lo):
        trial = lo + jnp.left_shift(jnp.int32(1), 30 - i)
        c = count_ge(_key_to_float(trial))
        return jnp.where(c >= k, trial, lo)

    lo = lax.fori_loop(0, 31, bit_body, lo)
    return _key_to_float(lo)


def _tri128():
    r = lax.broadcasted_iota(I32, (LANES, LANES), 0)
    c = lax.broadcasted_iota(I32, (LANES, LANES), 1)
    return jnp.where(r <= c, 1.0, 0.0).astype(BF16)


def _tie_select(eq, rem, tri):
    parts = []
    for c in range(eq.shape[1] // LANES):
        e = eq[:, c * LANES:(c + 1) * LANES]
        pre = _dot(jnp.where(e, 1.0, 0.0).astype(BF16), tri)
        parts.append(jnp.where(e & (pre <= rem), 1, 0))
        rem = rem - pre[:, LANES - 1:LANES]
    return jnp.concatenate(parts, axis=1), rem


def _dsa_query_prep(iq, smq):
    lane = lax.broadcasted_iota(I32, iq.shape, 1)
    iqh = []
    for h in range(IDX_HEADS):
        r = iq if h == 0 else pltpu.roll(iq, LANES - IDX_DIM * h, 1)
        iqh.append(jnp.where(lane < IDX_DIM, r, 0.0).astype(BF16))
    wcol = [smq[:, SM_IW + h:SM_IW + h + 1] for h in range(IDX_HEADS)]
    return iqh, wcol


def _dsa_scores(iqh, wcol, ik_blk, contract32=False):
    sc = None
    for h in range(IDX_HEADS):
        a = iqh[h][:, :IDX_DIM] if contract32 else iqh[h]
        z = _dot_nt(a, ik_blk)
        t = jnp.maximum(z, 0.0) * wcol[h]
        sc = t if sc is None else sc + t
    return sc


def _dsa_prompt_kernel(q_ref, iq_ref, smq_ref, k_ref, v_ref, smk_ref, o_ref,
                       sc_ref, m_ref, l_ref, acc_ref, ceq_ref, *, tq, tk, topk):
    j = pl.program_id(1)
    nkb = j + 1
    q0 = j * tq
    iqh, wcol = _dsa_query_prep(iq_ref[...], smq_ref[...])
    row = lax.broadcasted_iota(I32, (tq, tk), 0)
    col = lax.broadcasted_iota(I32, (tq, tk), 1)
    qchunk = (q0 + row) >> 6

    def visible(off):
        return ((off + col) >> 6) <= qchunk

    def score_body(kb, carry):
        off = pl.multiple_of(kb * tk, tk)
        ik = smk_ref[pl.ds(off, tk), :].astype(BF16)
        sc = _dsa_scores(iqh, wcol, ik)
        sc_ref[:, pl.ds(off, tk)] = jnp.where(visible(off), sc, -jnp.inf)
        return carry

    lax.fori_loop(0, nkb, score_body, 0)

    def count_cmp(trial, strict):
        def body(kb, c):
            off = pl.multiple_of(kb * tk, tk)
            s = sc_ref[:, pl.ds(off, tk)]
            hit = (s > trial) if strict else (s >= trial)
            return c + jnp.sum(jnp.where(hit, 1, 0), axis=1, keepdims=True)
        return lax.fori_loop(0, nkb, body, jnp.zeros((tq, 1), I32))

    thr = _kth_largest(lambda t: count_cmp(t, False), tq, topk)
    need = (topk - count_cmp(thr, True)).astype(F32)

    _flash_init(m_ref, l_ref, acc_ref)
    ceq_ref[...] = need
    qh = _split_heads_q(q_ref[...])
    tri = _tri128()

    def attend_body(kb, carry):
        off = pl.multiple_of(kb * tk, tk)
        s = sc_ref[:, pl.ds(off, tk)]
        tie, rem = _tie_select(s == thr, ceq_ref[...], tri)
        ceq_ref[...] = rem
        sel = visible(off) & ((s > thr) | (tie > 0))
        k_blk = k_ref[pl.ds(off, tk), :]
        v_blk = v_ref[pl.ds(off, tk), :]
        for h in range(A_HEADS):
            _flash_update(h, _dot_nt(qh[h], k_blk), sel, v_blk, m_ref, l_ref, acc_ref)
        return carry

    lax.fori_loop(0, nkb, attend_body, 0)
    o_ref[...] = _flash_finish(l_ref, acc_ref)


def _dsa_prompt(aq, iq, sm, akb, avb, *, batch, seq, tq, topk):
    nq = seq // tq
    tk = tq
    qrow = lambda w: pl.BlockSpec((tq, w), lambda b, j: (b * nq + j, 0))
    krow = lambda w: pl.BlockSpec((seq, w), lambda b, j: (b, 0))
    kern = functools.partial(_dsa_prompt_kernel, tq=tq, tk=tk, topk=topk)
    return pl.pallas_call(
        kern,
        grid=(batch, nq),
        in_specs=[qrow(HW), qrow(LANES), qrow(LANES), krow(HW), krow(HW), krow(LANES)],
        out_specs=qrow(HW),
        out_shape=jax.ShapeDtypeStruct((batch * seq, HW), BF16),
        scratch_shapes=[pltpu.VMEM((tq, seq), F32),
                        pltpu.VMEM((4, tq, 1), F32), pltpu.VMEM((4, tq, 1), F32),
                        pltpu.VMEM((4, tq, HW), F32), pltpu.VMEM((tq, 1), F32)],
        compiler_params=_cparams(("parallel", "arbitrary")),
        name="dsa_prompt",
    )(aq, iq, sm, akb, avb, sm)


def _dsa_sample_kernel(q_ref, iq_ref, smq_ref, kn_ref, vn_ref, ck_ref, cv_ref, cik_ref, o_ref,
                       *, tq, past, topk):
    pad = LANES - tq
    iqh, wcol = _dsa_query_prep(iq_ref[...], smq_ref[...])
    sc_c = _dsa_scores(iqh, wcol, cik_ref[0].astype(BF16), contract32=True)
    ik_new = jnp.concatenate([smq_ref[...].astype(BF16), jnp.zeros((pad, LANES), BF16)], axis=0)
    sc_n = _dsa_scores(iqh, wcol, ik_new)
    lane = lax.broadcasted_iota(I32, (tq, LANES), 1)
    sc_n = jnp.where(lane < tq, sc_n, -jnp.inf)
    sc = jnp.concatenate([sc_c, sc_n], axis=1)
    lk = past + LANES
    col = lax.broadcasted_iota(I32, (tq, lk), 1)
    vis = col < past + tq

    def count_ge(trial):
        return jnp.sum(jnp.where(sc >= trial, 1, 0), axis=1, keepdims=True)

    thr = _kth_largest(count_ge, tq, topk)
    n_gt = jnp.sum(jnp.where(sc > thr, 1, 0), axis=1, keepdims=True)
    need = (topk - n_gt).astype(F32)
    tie, _ = _tie_select(sc == thr, need, _tri128())
    sel = vis & ((sc > thr) | (tie > 0))

    k_all = jnp.concatenate([ck_ref[0].astype(BF16), kn_ref[...], jnp.zeros((pad, HW), BF16)], axis=0)
    v_all = jnp.concatenate([cv_ref[0].astype(BF16), vn_ref[...], jnp.zeros((pad, HW), BF16)], axis=0)
    qh = _split_heads_q(q_ref[...])
    o_ref[...] = _direct_attend([_dot_nt(qh[h], k_all) for h in range(A_HEADS)], sel, v_all)


def _dsa_sample(aq, iq, sm, akb, avb, cache_k, cache_v, cache_ik, *, n_prompt, dbatch, dseq, topk):
    past = cache_k.shape[1]
    base = n_prompt // dseq
    qrow = lambda w: pl.BlockSpec((dseq, w), lambda b: (base + b, 0))
    crow = lambda w: pl.BlockSpec((1, past, w), lambda b: (b, 0, 0))
    kern = functools.partial(_dsa_sample_kernel, tq=dseq, past=past, topk=topk)
    return pl.pallas_call(
        kern,
        grid=(dbatch,),
        in_specs=[qrow(HW), qrow(LANES), qrow(LANES), qrow(HW), qrow(HW),
                  crow(HW), crow(HW), crow(IDX_DIM)],
        out_specs=pl.BlockSpec((dseq, HW), lambda b: (b, 0)),
        out_shape=jax.ShapeDtypeStruct((dbatch * dseq, HW), BF16),
        compiler_params=_cparams(("parallel",)),
        name="dsa_sample",
    )(aq, iq, sm, akb, avb, cache_k, cache_v, cache_ik)


def _fox_prompt_kernel(q_ref, k_ref, v_ref, lft_ref, o_ref, ck_ref, m_ref, l_ref, acc_ref, *, tq, tk):
    j = pl.program_id(1)

    @pl.when(j == 0)
    def _():
        ck_ref[...] = _lane_cumsum(lft_ref[...])

    q0 = pl.multiple_of(j * tq, tq)
    row = lax.broadcasted_iota(I32, (tq, tk), 0)
    col = lax.broadcasted_iota(I32, (tq, tk), 1)
    qpos = q0 + row
    qh = _split_heads_q(q_ref[...])
    cref = [ck_ref[h:h + 1, pl.ds(q0, LANES)][:, 0:1] for h in range(B_HEADS)]
    _flash_init(m_ref, l_ref, acc_ref)

    def body(kb, carry):
        off = pl.multiple_of(kb * tk, tk)
        sel = (off + col) <= qpos
        k_blk = k_ref[pl.ds(off, tk), :]
        v_blk = v_ref[pl.ds(off, tk), :]
        for h in range(B_HEADS):
            bias = cref[h] - ck_ref[h:h + 1, pl.ds(off, tk)]
            _flash_update(h, _dot_nt(qh[h], k_blk) + bias, sel, v_blk, m_ref, l_ref, acc_ref)
        return carry

    lax.fori_loop(0, j + 1, body, 0)
    o_ref[...] = _flash_finish(l_ref, acc_ref)


def _fox_prompt(bq, bkb, bvb, lft, *, batch, seq, tq):
    nq = seq // tq
    tk = tq
    qrow = lambda w: pl.BlockSpec((tq, w), lambda b, j: (b * nq + j, 0))
    krow = lambda w: pl.BlockSpec((seq, w), lambda b, j: (b, 0))
    kern = functools.partial(_fox_prompt_kernel, tq=tq, tk=tk)
    return pl.pallas_call(
        kern,
        grid=(batch, nq),
        in_specs=[qrow(HW), krow(HW), krow(HW), pl.BlockSpec((8, seq), lambda b, j: (0, b))],
        out_specs=qrow(HW),
        out_shape=jax.ShapeDtypeStruct((batch * seq, HW), BF16),
        scratch_shapes=[pltpu.VMEM((8, seq), F32),
                        pltpu.VMEM((4, tq, 1), F32), pltpu.VMEM((4, tq, 1), F32),
                        pltpu.VMEM((4, tq, HW), F32)],
        compiler_params=_cparams(("parallel", "arbitrary")),
        name="fox_prompt",
    )(bq, bkb, bvb, lft)


def _fox_sample_kernel(q_ref, kn_ref, vn_ref, ck_ref, cv_ref, clf_ref, nlf_ref, o_ref, *, tq, past):
    pad = LANES - tq
    cum_c = _lane_cumsum(clf_ref[0])
    cum_n = _lane_cumsum(nlf_ref[0])
    bias_c = cum_c[:, past - 1:past] - cum_c
    bias_n = -cum_n
    bias = jnp.concatenate([bias_c, bias_n], axis=1)
    lk = past + LANES
    row = lax.broadcasted_iota(I32, (tq, lk), 0)
    col = lax.broadcasted_iota(I32, (tq, lk), 1)
    sel = col <= past + row
    k_all = jnp.concatenate([ck_ref[0].astype(BF16), kn_ref[...], jnp.zeros((pad, HW), BF16)], axis=0)
    v_all = jnp.concatenate([cv_ref[0].astype(BF16), vn_ref[...], jnp.zeros((pad, HW), BF16)], axis=0)
    qh = _split_heads_q(q_ref[...])
    logits = [_dot_nt(qh[h], k_all) + bias[h:h + 1, :] for h in range(B_HEADS)]
    o_ref[...] = _direct_attend(logits, sel, v_all)


def _fox_sample(bq, bkb, bvb, cache_k, cache_v, cache_lft, new_lft, *, n_prompt, dbatch, dseq):
    past = cache_k.shape[1]
    base = n_prompt // dseq
    qrow = lambda w: pl.BlockSpec((dseq, w), lambda b: (base + b, 0))
    crow = lambda w: pl.BlockSpec((1, past, w), lambda b: (b, 0, 0))
    kern = functools.partial(_fox_sample_kernel, tq=dseq, past=past)
    return pl.pallas_call(
        kern,
        grid=(dbatch,),
        in_specs=[qrow(HW), qrow(HW), qrow(HW), crow(HW), crow(HW),
                  pl.BlockSpec((1, 8, past), lambda b: (b, 0, 0)),
                  pl.BlockSpec((1, 8, LANES), lambda b: (b, 0, 0))],
        out_specs=pl.BlockSpec((dseq, HW), lambda b: (b, 0)),
        out_shape=jax.ShapeDtypeStruct((dbatch * dseq, HW), BF16),
        compiler_params=_cparams(("parallel",)),
        name="fox_sample",
    )(bq, bkb, bvb, cache_k, cache_v, cache_lft, new_lft)


def _mla_logits(q, kfull_blk):
    return [_dot_nt(q[:, h * LANES:(h + 1) * LANES], kfull_blk[:, h * LANES:(h + 1) * LANES])
            for h in range(C_HEADS)]


def _mla_prompt_kernel(q_ref, ckv_ref, smk_ref, w1_ref, w2_ref, o_ref,
                       kf_ref, vc_ref, m_ref, l_ref, acc_ref, *, tq, tk, seq):
    j = pl.program_id(1)

    @pl.when(j == 0)
    def _():
        rc = 512 if seq % 512 == 0 else tk
        for r in range(seq // rc):
            kv = (_dot(ckv_ref[r * rc:(r + 1) * rc, :].astype(BF16), w1_ref[...])
                  + _dot(smk_ref[r * rc:(r + 1) * rc, :].astype(BF16), w2_ref[...]))
            kf_ref[r * rc:(r + 1) * rc, :] = kv[:, :512].astype(BF16)
            vc_ref[r * rc:(r + 1) * rc, :] = kv[:, 512:].astype(BF16)

    q0 = j * tq
    row = lax.broadcasted_iota(I32, (tq, tk), 0)
    col = lax.broadcasted_iota(I32, (tq, tk), 1)
    qchunk = (q0 + row) >> 6
    q = q_ref[...]
    _flash_init(m_ref, l_ref, acc_ref)

    def body(kb, carry):
        off = pl.multiple_of(kb * tk, tk)
        sel = ((off + col) >> 6) <= qchunk
        kf = kf_ref[pl.ds(off, tk), :]
        v_blk = vc_ref[pl.ds(off, tk), :]
        for h, lg in enumerate(_mla_logits(q, kf)):
            _flash_update(h, lg, sel, v_blk, m_ref, l_ref, acc_ref)
        return carry

    lax.fori_loop(0, j + 1, body, 0)
    o_ref[...] = _flash_finish(l_ref, acc_ref)


def _mla_prompt(qc, ckv, sm, w1, w2, *, batch, seq, tq):
    nq = seq // tq
    tk = tq
    qrow = lambda w: pl.BlockSpec((tq, w), lambda b, j: (b * nq + j, 0))
    krow = lambda w: pl.BlockSpec((seq, w), lambda b, j: (b, 0))
    full = lambda a: pl.BlockSpec(a.shape, lambda b, j: (0,) * a.ndim)
    kern = functools.partial(_mla_prompt_kernel, tq=tq, tk=tk, seq=seq)
    return pl.pallas_call(
        kern,
        grid=(batch, nq),
        in_specs=[qrow(512), krow(LANES), krow(LANES), full(w1), full(w2)],
        out_specs=qrow(HW),
        out_shape=jax.ShapeDtypeStruct((batch * seq, HW), BF16),
        scratch_shapes=[pltpu.VMEM((seq, 512), BF16), pltpu.VMEM((seq, HW), BF16),
                        pltpu.VMEM((4, tq, 1), F32), pltpu.VMEM((4, tq, 1), F32),
                        pltpu.VMEM((4, tq, HW), F32)],
        compiler_params=_cparams(("parallel", "arbitrary")),
        name="mla_prompt",
    )(qc, ckv, sm, w1, w2)


def _mla_sample_kernel(q_ref, ckvn_ref, smn_ref, cckv_ref, ckr_ref, w1_ref, w2_ref, w2c_ref, o_ref,
                       *, tq, past):
    pad = LANES - tq
    kv_c = _dot(cckv_ref[0].astype(BF16), w1_ref[...]) + _dot(ckr_ref[0].astype(BF16), w2c_ref[...])
    kv_n = _dot(ckvn_ref[...].astype(BF16), w1_ref[...]) + _dot(smn_ref[...].astype(BF16), w2_ref[...])
    kv = jnp.concatenate([kv_c, kv_n, jnp.zeros((pad, 768), F32)], axis=0).astype(BF16)
    lk = past + LANES
    col = lax.broadcasted_iota(I32, (tq, lk), 1)
    sel = col < past + tq
    o_ref[...] = _direct_attend(_mla_logits(q_ref[...], kv[:, :512]), sel, kv[:, 512:])


def _mla_sample(qc, ckv, sm, cache_ckv, cache_kr, w1, w2, w2c, *, n_prompt, dbatch, dseq):
    past = cache_ckv.shape[1]
    base = n_prompt // dseq
    qrow = lambda w: pl.BlockSpec((dseq, w), lambda b: (base + b, 0))
    crow = lambda w: pl.BlockSpec((1, past, w), lambda b: (b, 0, 0))
    full = lambda a: pl.BlockSpec(a.shape, lambda b: (0,) * a.ndim)
    kern = functools.partial(_mla_sample_kernel, tq=dseq, past=past)
    return pl.pallas_call(
        kern,
        grid=(dbatch,),
        in_specs=[qrow(512), qrow(LANES), qrow(LANES), crow(C_KV_RANK), crow(C_ROPE),
                  full(w1), full(w2), full(w2c)],
        out_specs=pl.BlockSpec((dseq, HW), lambda b: (b, 0)),
        out_shape=jax.ShapeDtypeStruct((dbatch * dseq, HW), BF16),
        compiler_params=_cparams(("parallel",)),
        name="mla_sample",
    )(qc, ckv, sm, cache_ckv, cache_kr, w1, w2, w2c)


def _merge_kernel(x_ref, g1_ref, oa_ref, ob_ref, oc_ref, wg_ref, wua_ref, wub_ref, wuc_ref, wout_ref,
                  xo_ref):
    x = x_ref[...]
    d = x.shape[1]
    h = _rmsnorm(x, g1_ref[...]).astype(BF16)
    gates = jax.nn.sigmoid(_dot(h, wg_ref[...]))
    merged = (gates[:, 0:d] * _dot(oa_ref[...], wua_ref[...])
              + gates[:, d:2 * d] * _dot(ob_ref[...], wub_ref[...])
              + gates[:, 2 * d:3 * d] * _dot(oc_ref[...], wuc_ref[...]))
    xo_ref[...] = x + _dot(merged.astype(BF16), wout_ref[...])


def _merge(x, g1, oa, ob, oc, wg, wua, wub, wuc, wout, *, tm):
    n, d = x.shape
    row = lambda w: pl.BlockSpec((tm, w), lambda i: (i, 0))
    full = lambda a: pl.BlockSpec(a.shape, lambda i: (0,) * a.ndim)
    return pl.pallas_call(
        _merge_kernel,
        grid=(n // tm,),
        in_specs=[row(d), full(g1), row(HW), row(HW), row(HW),
                  full(wg), full(wua), full(wub), full(wuc), full(wout)],
        out_specs=row(d),
        out_shape=jax.ShapeDtypeStruct((n, d), F32),
        compiler_params=_cparams(("parallel",)),
        name="merge",
    )(x, g1, oa, ob, oc, wg, wua, wub, wuc, wout)


def _top_values(s, k):
    vals = []
    for _ in range(k):
        m = jnp.max(s, axis=0, keepdims=True)
        vals.append(m)
        s = jnp.where(s == m, -jnp.inf, s)
    return vals


def _peer_kernel(x_ref, g2_ref, wqt_ref, k1_ref, k2_ref, u_ref, vt_ref, o_ref,
                 ht_ref, acc_ref, s2_ref, e2_ref, th_ref, c_ref, v2_ref, cand_ref, g_ref,
                 *, nk, na, ncand):
    k = pl.program_id(1)
    tm = x_ref.shape[0]
    half = PEER_QDIM // 2

    @pl.when(k == 0)
    def _():
        h = _rmsnorm(x_ref[...], g2_ref[...])
        ht = h.T.astype(BF16)
        ht_ref[...] = ht
        qt = _dot(wqt_ref[...], ht)
        acc_ref[...] = jnp.zeros(acc_ref.shape, F32)
        for hh in range(PEER_HEADS):
            q1 = qt[hh * PEER_QDIM:hh * PEER_QDIM + half, :].astype(BF16)
            q2 = qt[hh * PEER_QDIM + half:(hh + 1) * PEER_QDIM, :].astype(BF16)
            s1 = _dot(k1_ref[hh * nk:(hh + 1) * nk, :], q1)
            s2 = _dot(k2_ref[hh * nk:(hh + 1) * nk, :], q2)
            v1 = _top_values(s1, PEER_TOPK)
            v2 = _top_values(s2, PEER_TOPK)
            for jj in range(PEER_TOPK):
                v2_ref[jj:jj + 1, :] = v2[jj]
            cand_ref[...] = jnp.full(cand_ref.shape, -jnp.inf, F32)
            off = 0
            for i in range(PEER_TOPK):
                nj = PEER_TOPK // (i + 1)
                cand_ref[off:off + nj, :] = v1[i] + v2_ref[0:nj, :]
                off += nj
            cand = cand_ref[...]
            tau = _top_values(cand, PEER_TOPK)[-1]
            top = v1[0] + v2[0]
            z = jnp.sum(jnp.where(cand >= tau, jnp.exp(cand - top), 0.0), axis=0, keepdims=True)
            th = jnp.full((nk, tm), jnp.inf, F32)
            for jj in range(PEER_TOPK):
                th = jnp.where((s1 + v2[jj]) >= tau, v2[jj], th)
            s2_ref[hh] = s2
            e2_ref[hh] = jnp.exp(s2 - v2[0])
            th_ref[:, hh, :] = th
            c_ref[:, hh, :] = jnp.exp(s1 - v1[0]) / z

    st = _dot(u_ref[...], ht_ref[...])
    act = jax.nn.gelu(st)
    a0 = pl.multiple_of(k * na, na)
    th = th_ref[pl.ds(a0, na)]
    cc = c_ref[pl.ds(a0, na)]
    for ai in range(na):
        w = None
        for hh in range(PEER_HEADS):
            t = jnp.where(s2_ref[hh] >= th[ai, hh:hh + 1, :], e2_ref[hh], 0.0) * cc[ai, hh:hh + 1, :]
            w = t if w is None else w + t
        g_ref[ai * nk:(ai + 1) * nk, :] = (w * act[ai * nk:(ai + 1) * nk, :]).astype(BF16)
    acc_ref[...] += _dot(vt_ref[...], g_ref[...])

    @pl.when(k == pl.num_programs(1) - 1)
    def _():
        o_ref[...] = x_ref[...] + acc_ref[...].T


def _peer(x, g2, wqt, k1, k2, u, vt, *, tm, na):
    n, d = x.shape
    nk = k1.shape[0] // PEER_HEADS
    te = na * nk
    nchunks = nk // na
    ncand = sum(PEER_TOPK // (i + 1) for i in range(PEER_TOPK))
    ncand = -(-ncand // 8) * 8
    full = lambda a: pl.BlockSpec(a.shape, lambda i, k: (0,) * a.ndim)
    kern = functools.partial(_peer_kernel, nk=nk, na=na, ncand=ncand)
    return pl.pallas_call(
        kern,
        grid=(n // tm, nchunks),
        in_specs=[pl.BlockSpec((tm, d), lambda i, k: (i, 0)), full(g2), full(wqt), full(k1), full(k2),
                  pl.BlockSpec((te, d), lambda i, k: (k, 0)),
                  pl.BlockSpec((d, te), lambda i, k: (0, k))],
        out_specs=pl.BlockSpec((tm, d), lambda i, k: (i, 0)),
        out_shape=jax.ShapeDtypeStruct((n, d), F32),
        scratch_shapes=[pltpu.VMEM((d, tm), BF16), pltpu.VMEM((d, tm), F32),
                        pltpu.VMEM((PEER_HEADS, nk, tm), F32), pltpu.VMEM((PEER_HEADS, nk, tm), F32),
                        pltpu.VMEM((nk, PEER_HEADS, tm), F32), pltpu.VMEM((nk, PEER_HEADS, tm), F32),
                        pltpu.VMEM((PEER_TOPK, tm), F32), pltpu.VMEM((ncand, tm), F32),
                        pltpu.VMEM((te, tm), BF16)],
        compiler_params=_cparams(("parallel", "arbitrary")),
        name="peer",
    )(x, g2, wqt, k1, k2, u, vt)


def _final_norm_kernel(x_ref, g_ref, o_ref):
    o_ref[...] = _rmsnorm(x_ref[...], g_ref[...])


def _final_norm(x, g, *, tm):
    n, d = x.shape
    return pl.pallas_call(
        _final_norm_kernel,
        grid=(n // tm,),
        in_specs=[pl.BlockSpec((tm, d), lambda i: (i, 0)), pl.BlockSpec((1, d), lambda i: (0, 0))],
        out_specs=pl.BlockSpec((tm, d), lambda i: (i, 0)),
        out_shape=jax.ShapeDtypeStruct((n, d), F32),
        compiler_params=_cparams(("parallel",)),
        name="final_norm",
    )(x, g)


def _rope_tables(pos):
    rows = pos.shape[0]

    def cs(half):
        inv = jnp.power(ROPE_THETA, -jnp.arange(half, dtype=F32) / half)
        ang = pos.astype(F32)[:, None] * inv[None, :]
        c, s = jnp.cos(ang), jnp.sin(ang)
        return jnp.concatenate([c, c], axis=1), jnp.concatenate([-s, s], axis=1)

    c64, s64 = cs(HEAD_DIM // 2)
    c32, s32 = cs(IDX_DIM // 2)
    one32 = jnp.ones((rows, 32), F32)
    zero32 = jnp.zeros((rows, 32), F32)
    c_sm = jnp.concatenate([c32, c32, one32, one32], axis=1)
    s_sm = jnp.concatenate([s32, s32, zero32, zero32], axis=1)
    c_qc = jnp.concatenate([one32, one32, c32, one32], axis=1)
    s_qc = jnp.concatenate([zero32, zero32, s32, zero32], axis=1)
    ctab = jnp.concatenate([jnp.tile(c64, (1, 4)), jnp.tile(c32, (1, 4)), c_sm, jnp.tile(c_qc, (1, 4))], axis=1)
    stab = jnp.concatenate([jnp.tile(s64, (1, 4)), jnp.tile(s32, (1, 4)), s_sm, jnp.tile(s_qc, (1, 4))], axis=1)
    return ctab, stab


def _split_w_in(w_in):
    sizes = (HW, HW, HW, IDX_HEADS * IDX_DIM, IDX_DIM, IDX_HEADS, HW, HW, HW, B_HEADS,
             C_Q_RANK, C_KV_RANK, C_ROPE)
    offs = np.cumsum((0,) + sizes)
    parts = [w_in[..., int(offs[i]):int(offs[i + 1])] for i in range(len(sizes))]
    gates = w_in[..., int(offs[-1]):]
    return parts, gates


def _pack_weights(w_in, w_uq, w_ukv, b_forget):
    depth, d, _ = w_in.shape
    (a_q, a_k, a_v, i_q, i_k, i_w, b_q, b_k, b_v, b_f, c_q, c_kv, c_kr), gates = _split_w_in(w_in)
    z = lambda w: jnp.zeros((depth, d, w), F32)
    small = jnp.concatenate([i_k, c_kr, b_f, z(4), i_w, z(LANES - SM_IW - IDX_HEADS)], axis=-1)
    wmain = jnp.concatenate([a_q, a_k, a_v, b_q, b_k, b_v, c_q, c_kv, i_q, small], axis=-1).astype(BF16)
    wg = gates.astype(BF16)

    uq = w_uq.reshape(depth, C_Q_RANK, C_HEADS, C_NOPE + C_ROPE)
    uq = jnp.concatenate([uq, jnp.zeros((depth, C_Q_RANK, C_HEADS, 32), F32)], axis=-1)
    wuq = uq.reshape(depth, C_Q_RANK, C_HEADS * LANES).astype(BF16)

    ukv = w_ukv.reshape(depth, C_KV_RANK, C_HEADS, C_NOPE + C_VDIM)
    kn = jnp.concatenate([ukv[..., :C_NOPE], jnp.zeros((depth, C_KV_RANK, C_HEADS, 64), F32)], axis=-1)
    w1 = jnp.concatenate([kn.reshape(depth, C_KV_RANK, 512),
                          ukv[..., C_NOPE:].reshape(depth, C_KV_RANK, HW)], axis=-1).astype(BF16)
    eye = np.zeros((C_ROPE, 768), np.float32)
    for h in range(C_HEADS):
        for r in range(C_ROPE):
            eye[r, h * LANES + C_NOPE + r] = 1.0
    w2c = jnp.asarray(eye).astype(BF16)
    w2_np = np.zeros((LANES, 768), np.float32)
    w2_np[SM_KR:SM_KR + C_ROPE] = eye
    w2 = jnp.asarray(w2_np).astype(BF16)

    aux = jnp.zeros((depth, 8, LANES), F32)
    aux = aux.at[:, 0, SM_BF:SM_BF + B_HEADS].set(b_forget)
    scale = np.ones((LANES,), np.float32)
    scale[SM_IW:SM_IW + IDX_HEADS] = (IDX_HEADS * IDX_DIM) ** -0.5
    aux = aux.at[:, 1, :].set(jnp.asarray(scale))
    return wmain, wg, wuq, w1, w2, w2c, aux


def _tile_rows(n):
    for t in (512, 256, 128):
        if n % t == 0:
            return t
    raise ValueError(f"token count {n} is not a multiple of 128")


def kernel(x_prompt, x_sample, cache_k_a, cache_v_a, cache_kidx_a, cache_k_b, cache_v_b, cache_logf_b,
           cache_ckv_c, cache_krope_c, norm1_g, w_in, c_q_norm_g, c_kv_norm_g, w_uq, w_ukv, b_forget,
           w_up_a, w_up_b, w_up_c, w_out, norm2_g, peer_w_q, peer_keys, peer_u, peer_v, final_norm_g):
    batch, seq, d = x_prompt.shape
    dbatch, dseq, _ = x_sample.shape
    depth = w_in.shape[0]
    past = cache_k_a.shape[2]
    n_p = batch * seq
    n_s = dbatch * dseq
    n = n_p + n_s
    nk = peer_keys.shape[3]
    assert dseq == CHUNK and past % LANES == 0 and seq % 256 == 0

    tm = min(_tile_rows(seq), _tile_rows(n_s))
    tq = 256
    topk_p = min(TOPK_MAX, seq // 4)
    topk_s = min(TOPK_MAX, (past + dseq) // 4)

    pos = jnp.concatenate([jnp.arange(seq, dtype=jnp.int32),
                           past + (jnp.arange(tm, dtype=jnp.int32) % dseq)])
    ctab, stab = _rope_tables(pos)

    wmain, wg, wuq, w1, w2, w2c, aux = _pack_weights(w_in, w_uq, w_ukv, b_forget)
    wua, wub, wuc, wout = (w.astype(BF16) for w in (w_up_a, w_up_b, w_up_c, w_out))
    wqt = jnp.swapaxes(peer_w_q, 1, 2).astype(BF16)
    half = PEER_QDIM // 2
    k1 = peer_keys[:, :, 0].reshape(depth, PEER_HEADS * nk, half).astype(BF16)
    k2 = peer_keys[:, :, 1].reshape(depth, PEER_HEADS * nk, half).astype(BF16)
    pu = peer_u.astype(BF16)
    pvt = jnp.swapaxes(peer_v, 1, 2).astype(BF16)
    clf_t = jnp.swapaxes(cache_logf_b, 2, 3)
    clf_t = jnp.concatenate([clf_t, jnp.zeros_like(clf_t)], axis=2)

    g_row = lambda g: g.reshape(1, -1)
    x = jnp.concatenate([x_prompt.reshape(n_p, d), x_sample.reshape(n_s, d)], axis=0)
    rows_p = [[] for _ in range(8)]
    rows_s = [[] for _ in range(8)]
    na = 8 if nk % 8 == 0 else nk

    for l in range(depth):
        g1 = g_row(norm1_g[l])
        (aq, akf, akb, avf, avb, bq, bkf, bkb, bvf, bvb, qc, ckv, iq, sm, lft) = _inproj(
            x, g1, wmain[l], wuq[l], g_row(c_q_norm_g[l]), g_row(c_kv_norm_g[l]), aux[l], ctab, stab,
            tm=tm, n_prompt=n_p, seq=seq)

        new_lft = lft[:, n_p:].reshape(8, dbatch, dseq).transpose(1, 0, 2)
        new_lft = jnp.concatenate([new_lft, jnp.zeros((dbatch, 8, LANES - dseq), F32)], axis=2)

        oa_p = _dsa_prompt(aq, iq, sm, akb, avb, batch=batch, seq=seq, tq=tq, topk=topk_p)
        oa_s = _dsa_sample(aq, iq, sm, akb, avb,
                           cache_k_a[l].reshape(dbatch, past, HW), cache_v_a[l].reshape(dbatch, past, HW),
                           cache_kidx_a[l], n_prompt=n_p, dbatch=dbatch, dseq=dseq, topk=topk_s)
        ob_p = _fox_prompt(bq, bkb, bvb, lft, batch=batch, seq=seq, tq=tq)
        ob_s = _fox_sample(bq, bkb, bvb,
                           cache_k_b[l].reshape(dbatch, past, HW), cache_v_b[l].reshape(dbatch, past, HW),
                           clf_t[l], new_lft, n_prompt=n_p, dbatch=dbatch, dseq=dseq)
        oc_p = _mla_prompt(qc, ckv, sm, w1[l], w2, batch=batch, seq=seq, tq=tq)
        oc_s = _mla_sample(qc, ckv, sm, cache_ckv_c[l], cache_krope_c[l], w1[l], w2, w2c,
                           n_prompt=n_p, dbatch=dbatch, dseq=dseq)
        oa = jnp.concatenate([oa_p, oa_s], axis=0)
        ob = jnp.concatenate([ob_p, ob_s], axis=0)
        oc = jnp.concatenate([oc_p, oc_s], axis=0)

        x = _merge(x, g1, oa, ob, oc, wg[l], wua[l], wub[l], wuc[l], wout[l], tm=tm)
        x = _peer(x, g_row(norm2_g[l]), wqt[l], k1[l], k2[l], pu[l], pvt[l], tm=tm, na=na)

        new = (akf, avf, sm[:, SM_IK:SM_IK + IDX_DIM], bkf, bvf, sm[:, SM_BF:SM_BF + B_HEADS],
               ckv, sm[:, SM_KR:SM_KR + C_ROPE])
        for i, a in enumerate(new):
            rows_p[i].append(a[:n_p])
            rows_s[i].append(a[n_p:])

    y = _final_norm(x, g_row(final_norm_g), tm=tm)
    y_p = y[:n_p].reshape(batch, seq, d)
    y_s = y[n_p:].reshape(dbatch, dseq, d)

    tails = [(A_HEADS, HEAD_DIM), (A_HEADS, HEAD_DIM), (IDX_DIM,), (B_HEADS, HEAD_DIM), (B_HEADS, HEAD_DIM),
             (B_HEADS,), (C_KV_RANK,), (C_ROPE,)]
    outs_p = [jnp.stack(r).reshape((depth, batch, seq) + t) for r, t in zip(rows_p, tails)]
    outs_s = [jnp.stack(r).reshape((depth, dbatch, dseq) + t) for r, t in zip(rows_s, tails)]
    return (y_p, y_s, *outs_p, *outs_s)
```

```python
import functools

import jax
import jax.numpy as jnp
import numpy as np
from jax import lax
from jax.experimental import pallas as pl
from jax.experimental.pallas import tpu as pltpu

F32 = jnp.float32
BF16 = jnp.bfloat16
I32 = jnp.int32

CHUNK = 64
HEAD_DIM = 64
ROPE_THETA = 10000.0
NORM_EPS = 1e-6
A_HEADS = 4
IDX_HEADS = 4
IDX_DIM = 32
TOPK_MAX = 256
B_HEADS = 4
C_HEADS = 4
C_NOPE = 64
C_ROPE = 32
C_VDIM = 64
C_Q_RANK = 256
C_KV_RANK = 128
N_BRANCHES = 3
PEER_HEADS = 8
PEER_QDIM = 128
PEER_TOPK = 16

HW = A_HEADS * HEAD_DIM
LANES = 128
VMEM_LIMIT = 56 * 1024 * 1024

SM_IK = 0
SM_KR = 32
SM_BF = 64
SM_IW = 72

MC_AQ, MC_AK, MC_AV = 0, 256, 512
MC_BQ, MC_BK, MC_BV = 768, 1024, 1280
MC_CQ = 1536
MC_CKV = 1792
MC_IQ = 1920
MC_SM = 2048
MAIN_COLS = 2176

RT_64 = 0
RT_IQ = 256
RT_SM = 384
RT_QC = 512
RT_COLS = 1024

NEG = -1e30
KEY_NEG_INF = -2139095041
INT_MIN = -2147483648


def _cparams(sem):
    return pltpu.CompilerParams(dimension_semantics=sem, vmem_limit_bytes=VMEM_LIMIT)


def _rmsnorm(x, g):
    ms = jnp.mean(x * x, axis=-1, keepdims=True)
    return x * lax.rsqrt(ms + NORM_EPS) * g


def _dot(a, b):
    return jnp.dot(a, b, preferred_element_type=F32)


def _dot_nt(a, b):
    return lax.dot_general(a, b, (((1,), (1,)), ((), ())), preferred_element_type=F32)


def _rope(v, c, s, half):
    w = v.shape[-1]
    lane = lax.broadcasted_iota(I32, v.shape, 1)
    first = (lane & (2 * half - 1)) < half
    vr = jnp.where(first, pltpu.roll(v, w - half, 1), pltpu.roll(v, half, 1))
    return v * c + vr * s


def _inproj_kernel(x_ref, g1_ref, wmain_ref, wuq_ref, gq_ref, gkv_ref, aux_ref, ct_ref, st_ref,
                   aq_ref, akf_ref, akb_ref, avf_ref, avb_ref,
                   bq_ref, bkf_ref, bkb_ref, bvf_ref, bvb_ref,
                   qc_ref, ckv_ref, iq_ref, sm_ref, lft_ref,
                   aqt_ref, avt_ref, bqt_ref, bvt_ref, qct_ref, iqt_ref, ckvt_ref, iwt_ref):
    x = x_ref[...]
    h = _rmsnorm(x, g1_ref[...]).astype(BF16)
    y = _dot(h, wmain_ref[...])
    ct = ct_ref[...]
    st = st_ref[...]

    c64 = ct[:, RT_64:RT_64 + HW]
    s64 = st[:, RT_64:RT_64 + HW]
    aq = _rope(y[:, MC_AQ:MC_AQ + HW], c64, s64, HEAD_DIM // 2)
    aq = aq * (HEAD_DIM ** -0.5)
    aq_ref[...] = aq.astype(BF16)
    aqt_ref[...] = aq.T.astype(BF16)
    ak = _rope(y[:, MC_AK:MC_AK + HW], c64, s64, HEAD_DIM // 2)
    akf_ref[...] = ak
    akb_ref[...] = ak.astype(BF16)
    av = y[:, MC_AV:MC_AV + HW]
    avf_ref[...] = av
    avb_ref[...] = av.astype(BF16)
    avt_ref[...] = av.T.astype(BF16)

    bq = y[:, MC_BQ:MC_BQ + HW] * (HEAD_DIM ** -0.5)
    bq_ref[...] = bq.astype(BF16)
    bqt_ref[...] = bq.T.astype(BF16)
    bk = y[:, MC_BK:MC_BK + HW]
    bkf_ref[...] = bk
    bkb_ref[...] = bk.astype(BF16)
    bv = y[:, MC_BV:MC_BV + HW]
    bvf_ref[...] = bv
    bvb_ref[...] = bv.astype(BF16)
    bvt_ref[...] = bv.T.astype(BF16)

    cq = _rmsnorm(y[:, MC_CQ:MC_CQ + C_Q_RANK], gq_ref[...]).astype(BF16)
    qc = _dot(cq, wuq_ref[...])
    qc = _rope(qc, ct[:, RT_QC:RT_QC + 512], st[:, RT_QC:RT_QC + 512], C_ROPE // 2)
    qc = qc * ((C_NOPE + C_ROPE) ** -0.5)
    qc_ref[...] = qc.astype(BF16)
    qct_ref[...] = qc.T.astype(BF16)

    ckv = _rmsnorm(y[:, MC_CKV:MC_CKV + C_KV_RANK], gkv_ref[...])
    ckv_ref[...] = ckv
    ckvt_ref[...] = ckv.T.astype(BF16)

    iq = _rope(y[:, MC_IQ:MC_IQ + LANES], ct[:, RT_IQ:RT_IQ + LANES],
               st[:, RT_IQ:RT_IQ + LANES], IDX_DIM // 2)
    iq_ref[...] = iq
    iqt_ref[...] = iq.T.astype(BF16)

    sm = _rope(y[:, MC_SM:MC_SM + LANES], ct[:, RT_SM:RT_SM + LANES],
               st[:, RT_SM:RT_SM + LANES], IDX_DIM // 2)
    lane = lax.broadcasted_iota(I32, sm.shape, 1)
    is_f = (lane >= SM_BF) & (lane < SM_BF + B_HEADS)
    z = sm + aux_ref[0:1, :]
    logsig = jnp.minimum(z, 0.0) - jnp.log1p(jnp.exp(-jnp.abs(z)))
    sm = jnp.where(is_f, logsig, sm * aux_ref[1:2, :])
    sm_ref[...] = sm
    smt = sm.T
    lft_ref[...] = smt[SM_BF:SM_BF + 8, :]
    iwt_ref[...] = smt[SM_IW:SM_IW + 8, :]


def _inproj(x, g1, wmain, wuq, gq, gkv, aux, ctab, stab, *, tm, n_prompt, seq):
    n, d = x.shape
    nt = n // tm
    npt = n_prompt // tm
    tps = seq // tm

    def tmap(i):
        return (jnp.where(i < npt, i % tps, tps), 0)

    row = lambda w: pl.BlockSpec((tm, w), lambda i: (i, 0))
    full = lambda a: pl.BlockSpec(a.shape, lambda i: (0,) * a.ndim)
    out_shapes = [
        jax.ShapeDtypeStruct((n, HW), BF16),
        jax.ShapeDtypeStruct((n, HW), F32),
        jax.ShapeDtypeStruct((n, HW), BF16),
        jax.ShapeDtypeStruct((n, HW), F32),
        jax.ShapeDtypeStruct((n, HW), BF16),
        jax.ShapeDtypeStruct((n, HW), BF16),
        jax.ShapeDtypeStruct((n, HW), F32),
        jax.ShapeDtypeStruct((n, HW), BF16),
        jax.ShapeDtypeStruct((n, HW), F32),
        jax.ShapeDtypeStruct((n, HW), BF16),
        jax.ShapeDtypeStruct((n, 512), BF16),
        jax.ShapeDtypeStruct((n, LANES), F32),
        jax.ShapeDtypeStruct((n, LANES), F32),
        jax.ShapeDtypeStruct((n, LANES), F32),
        jax.ShapeDtypeStruct((8, n), F32),
        jax.ShapeDtypeStruct((HW, n), BF16),
        jax.ShapeDtypeStruct((HW, n), BF16),
        jax.ShapeDtypeStruct((HW, n), BF16),
        jax.ShapeDtypeStruct((HW, n), BF16),
        jax.ShapeDtypeStruct((512, n), BF16),
        jax.ShapeDtypeStruct((LANES, n), BF16),
        jax.ShapeDtypeStruct((LANES, n), BF16),
        jax.ShapeDtypeStruct((8, n), F32),
    ]
    col = lambda w: pl.BlockSpec((w, tm), lambda i: (0, i))
    out_specs = [row(HW)] * 10 + [row(512), row(LANES), row(LANES), row(LANES), col(8),
                                  col(HW), col(HW), col(HW), col(HW), col(512), col(LANES), col(LANES),
                                  col(8)]
    return pl.pallas_call(
        _inproj_kernel,
        grid=(nt,),
        in_specs=[row(d), full(g1), full(wmain), full(wuq), full(gq), full(gkv), full(aux),
                  pl.BlockSpec((tm, RT_COLS), tmap), pl.BlockSpec((tm, RT_COLS), tmap)],
        out_specs=out_specs,
        out_shape=out_shapes,
        compiler_params=_cparams(("parallel",)),
        name="inproj",
    )(x, g1, wmain, wuq, gq, gkv, aux, ctab, stab)


def _head_lane_mask(shape, h, width=HEAD_DIM):
    lane = lax.broadcasted_iota(I32, shape, 1)
    return (lane >= h * width) & (lane < (h + 1) * width)


def _split_heads_q(q):
    return [jnp.where(_head_lane_mask(q.shape, h), q, jnp.zeros_like(q)) for h in range(4)]


def _flash_init(m_ref, l_ref, acc_ref):
    m_ref[...] = jnp.full(m_ref.shape, NEG, F32)
    l_ref[...] = jnp.zeros(l_ref.shape, F32)
    acc_ref[...] = jnp.zeros(acc_ref.shape, F32)


def _flash_update(h, logits, sel, v_blk, m_ref, l_ref, acc_ref):
    lm = jnp.where(sel, logits, NEG)
    m_old = m_ref[h]
    m_new = jnp.maximum(m_old, jnp.max(lm, axis=1, keepdims=True))
    p = jnp.where(sel, jnp.exp(logits - m_new), 0.0)
    alpha = jnp.exp(m_old - m_new)
    l_ref[h] = alpha * l_ref[h] + jnp.sum(p, axis=1, keepdims=True)
    acc_ref[h] = alpha * acc_ref[h] + _dot(p.astype(BF16), v_blk)
    m_ref[h] = m_new


def _flash_finish(l_ref, acc_ref):
    out = None
    for h in range(4):
        o = acc_ref[h] / l_ref[h]
        o = jnp.where(_head_lane_mask(o.shape, h), o, 0.0)
        out = o if out is None else out + o
    return out.astype(BF16)


def _direct_attend(logits_list, sel, v_blk):
    out = None
    for h, logits in enumerate(logits_list):
        lm = jnp.where(sel, logits, NEG)
        m = jnp.max(lm, axis=1, keepdims=True)
        p = jnp.where(sel, jnp.exp(logits - m), 0.0)
        l = jnp.sum(p, axis=1, keepdims=True)
        o = _dot(p.astype(BF16), v_blk) / l
        o = jnp.where(_head_lane_mask(o.shape, h), o, 0.0)
        out = o if out is None else out + o
    return out.astype(BF16)


def _lane_cumsum(x):
    n = x.shape[1]
    lane = lax.broadcasted_iota(I32, x.shape, 1)
    k = 1
    while k < n:
        x = x + jnp.where(lane >= k, pltpu.roll(x, k, 1), 0.0)
        k *= 2
    return x


def _key_to_float(key):
    bits = key ^ ((key >> 31) & 0x7FFFFFFF)
    return jnp.where(key < KEY_NEG_INF, -jnp.inf, pltpu.bitcast(bits, F32))


def _kth_largest(count_ge, shape, k):
    zero = jnp.zeros(shape, I32)
    c0 = count_ge(_key_to_float(zero))
    lo = jnp.where(c0 >= k, zero, jnp.full(shape, INT_MIN, I32))

    def bit_body(i, lo):
        trial = lo + jnp.left_shift(jnp.int32(1), 30 - i)
        c = count_ge(_key_to_float(trial))
        return jnp.where(c >= k, trial, lo)

    lo = lax.fori_loop(0, 31, bit_body, lo)
    return _key_to_float(lo)


def _tri128():
    r = lax.broadcasted_iota(I32, (LANES, LANES), 0)
    c = lax.broadcasted_iota(I32, (LANES, LANES), 1)
    return jnp.where(r <= c, 1.0, 0.0).astype(BF16)


def _tri(n):
    r = lax.broadcasted_iota(I32, (n, n), 0)
    c = lax.broadcasted_iota(I32, (n, n), 1)
    return jnp.where(c <= r, 1.0, 0.0).astype(BF16)


def _split_heads_qt(qt):
    row = lax.broadcasted_iota(I32, qt.shape, 0)
    return [jnp.where((row >= h * HEAD_DIM) & (row < (h + 1) * HEAD_DIM), qt, jnp.zeros_like(qt))
            for h in range(4)]


def _flash_t(nkb, tk, tq, logits_fn, sel_fn, vt_ref, acc_ref):
    acc_ref[...] = jnp.zeros(acc_ref.shape, F32)

    def body(kb, carry):
        ms, ls = carry
        off = pl.multiple_of(kb * tk, tk)
        sel = sel_fn(off)
        lgs = logits_fn(off)
        new_m, new_l = [], []
        for h in range(4):
            lm = jnp.where(sel, lgs[h], NEG)
            m_new = jnp.maximum(ms[h], jnp.max(lm, axis=0, keepdims=True))
            p = jnp.exp(lm - m_new)
            alpha = jnp.exp(ms[h] - m_new)
            new_l.append(alpha * ls[h] + jnp.sum(p, axis=0, keepdims=True))
            rows = slice(h * HEAD_DIM, (h + 1) * HEAD_DIM)
            acc_ref[rows, :] = alpha * acc_ref[rows, :] + _dot(vt_ref[rows, pl.ds(off, tk)], p.astype(BF16))
            new_m.append(m_new)
        return tuple(new_m), tuple(new_l)

    init = (tuple(jnp.full((1, tq), NEG / 2, F32) for _ in range(4)),
            tuple(jnp.zeros((1, tq), F32) for _ in range(4)))
    _, ls = lax.fori_loop(0, nkb, body, init)
    out = jnp.concatenate([acc_ref[h * HEAD_DIM:(h + 1) * HEAD_DIM, :] / ls[h] for h in range(4)], axis=0)
    return out.T.astype(BF16)


def _tie_select(eq, rem, tri):
    parts = []
    for c in range(eq.shape[1] // LANES):
        e = eq[:, c * LANES:(c + 1) * LANES]
        pre = _dot(jnp.where(e, 1.0, 0.0).astype(BF16), tri)
        parts.append(jnp.where(e & (pre <= rem), 1, 0))
        rem = rem - pre[:, LANES - 1:LANES]
    return jnp.concatenate(parts, axis=1), rem


def _dsa_query_prep(iq, smq):
    lane = lax.broadcasted_iota(I32, iq.shape, 1)
    iqh = []
    for h in range(IDX_HEADS):
        r = iq if h == 0 else pltpu.roll(iq, LANES - IDX_DIM * h, 1)
        iqh.append(jnp.where(lane < IDX_DIM, r, 0.0).astype(BF16))
    wcol = [smq[:, SM_IW + h:SM_IW + h + 1] for h in range(IDX_HEADS)]
    return iqh, wcol


def _dsa_scores(iqh, wcol, ik_blk, contract32=False):
    sc = None
    for h in range(IDX_HEADS):
        a = iqh[h][:, :IDX_DIM] if contract32 else iqh[h]
        z = _dot_nt(a, ik_blk)
        t = jnp.maximum(z, 0.0) * wcol[h]
        sc = t if sc is None else sc + t
    return sc


def _dsa_prompt_kernel(qt_ref, iqt_ref, iwt_ref, k_ref, vt_ref, smk_ref, o_ref,
                       sc_ref, acc_ref, rem_ref, *, tq, tk, topk):
    j = pl.program_id(1)
    nkb = j + 1
    q0 = j * tq
    iqt = iqt_ref[...]
    pad = jnp.zeros((LANES - IDX_DIM, tq), BF16)
    iqh = [jnp.concatenate([iqt[IDX_DIM * h:IDX_DIM * (h + 1), :], pad], axis=0) for h in range(IDX_HEADS)]
    iw = iwt_ref[...]
    row = lax.broadcasted_iota(I32, (tk, tq), 0)
    col = lax.broadcasted_iota(I32, (tk, tq), 1)
    qchunk = (q0 + col) >> 6

    def visible(off):
        return ((off + row) >> 6) <= qchunk

    def score_body(kb, carry):
        off = pl.multiple_of(kb * tk, tk)
        ik = smk_ref[pl.ds(off, tk), :].astype(BF16)
        sc = None
        for h in range(IDX_HEADS):
            t = jnp.maximum(_dot(ik, iqh[h]), 0.0) * iw[h:h + 1, :]
            sc = t if sc is None else sc + t
        sc_ref[pl.ds(off, tk), :] = jnp.where(visible(off), sc, -jnp.inf)
        return carry

    lax.fori_loop(0, nkb, score_body, 0)

    def count_cmp(trial, strict):
        def body(kb, c):
            off = pl.multiple_of(kb * tk, tk)
            s = sc_ref[pl.ds(off, tk), :]
            hit = (s > trial) if strict else (s >= trial)
            return c + jnp.sum(jnp.where(hit, 1, 0), axis=0, keepdims=True)
        return lax.fori_loop(0, nkb, body, jnp.zeros((1, tq), I32))

    thr = _kth_largest(lambda t: count_cmp(t, False), (1, tq), topk)
    rem_ref[0:1, :] = (topk - count_cmp(thr, True)).astype(F32)
    qh = _split_heads_qt(qt_ref[...])
    tri = _tri(tk)

    def sel_fn(off):
        s = sc_ref[pl.ds(off, tk), :]
        eq = s == thr
        pre = _dot(tri, jnp.where(eq, 1.0, 0.0).astype(BF16))
        rem = rem_ref[0:1, :]
        keep = eq & (pre <= rem)
        rem_ref[0:1, :] = rem - pre[tk - 1:tk, :]
        return visible(off) & ((s > thr) | keep)

    def logits_fn(off):
        k_blk = k_ref[pl.ds(off, tk), :]
        return [_dot(k_blk, qh[h]) for h in range(A_HEADS)]

    o_ref[...] = _flash_t(nkb, tk, tq, logits_fn, sel_fn, vt_ref, acc_ref)


def _dsa_prompt(aqt, iqt, iwt, akb, avt, sm, *, batch, seq, tq, topk):
    nq = seq // tq
    tk = tq
    qcol = lambda w: pl.BlockSpec((w, tq), lambda b, j: (0, b * nq + j))
    krow = lambda w: pl.BlockSpec((seq, w), lambda b, j: (b, 0))
    kern = functools.partial(_dsa_prompt_kernel, tq=tq, tk=tk, topk=topk)
    return pl.pallas_call(
        kern,
        grid=(batch, nq),
        in_specs=[qcol(HW), qcol(LANES), qcol(8), krow(HW),
                  pl.BlockSpec((HW, seq), lambda b, j: (0, b)), krow(LANES)],
        out_specs=pl.BlockSpec((tq, HW), lambda b, j: (b * nq + j, 0)),
        out_shape=jax.ShapeDtypeStruct((batch * seq, HW), BF16),
        scratch_shapes=[pltpu.VMEM((seq, tq), F32), pltpu.VMEM((HW, tq), F32), pltpu.VMEM((8, tq), F32)],
        compiler_params=_cparams(("parallel", "arbitrary")),
        name="dsa_prompt",
    )(aqt, iqt, iwt, akb, avt, sm)


def _dsa_sample_kernel(q_ref, iq_ref, smq_ref, kn_ref, vn_ref, ck_ref, cv_ref, cik_ref, o_ref,
                       *, tq, past, topk):
    pad = LANES - tq
    iqh, wcol = _dsa_query_prep(iq_ref[...], smq_ref[...])
    sc_c = _dsa_scores(iqh, wcol, cik_ref[0].astype(BF16), contract32=True)
    ik_new = jnp.concatenate([smq_ref[...].astype(BF16), jnp.zeros((pad, LANES), BF16)], axis=0)
    sc_n = _dsa_scores(iqh, wcol, ik_new)
    lane = lax.broadcasted_iota(I32, (tq, LANES), 1)
    sc_n = jnp.where(lane < tq, sc_n, -jnp.inf)
    sc = jnp.concatenate([sc_c, sc_n], axis=1)
    lk = past + LANES
    col = lax.broadcasted_iota(I32, (tq, lk), 1)
    vis = col < past + tq

    def count_ge(trial):
        return jnp.sum(jnp.where(sc >= trial, 1, 0), axis=1, keepdims=True)

    thr = _kth_largest(count_ge, (tq, 1), topk)
    n_gt = jnp.sum(jnp.where(sc > thr, 1, 0), axis=1, keepdims=True)
    need = (topk - n_gt).astype(F32)
    tie, _ = _tie_select(sc == thr, need, _tri128())
    sel = vis & ((sc > thr) | (tie > 0))

    k_all = jnp.concatenate([ck_ref[0].astype(BF16), kn_ref[...], jnp.zeros((pad, HW), BF16)], axis=0)
    v_all = jnp.concatenate([cv_ref[0].astype(BF16), vn_ref[...], jnp.zeros((pad, HW), BF16)], axis=0)
    qh = _split_heads_q(q_ref[...])
    o_ref[...] = _direct_attend([_dot_nt(qh[h], k_all) for h in range(A_HEADS)], sel, v_all)


def _dsa_sample(aq, iq, sm, akb, avb, cache_k, cache_v, cache_ik, *, n_prompt, dbatch, dseq, topk):
    past = cache_k.shape[1]
    base = n_prompt // dseq
    qrow = lambda w: pl.BlockSpec((dseq, w), lambda b: (base + b, 0))
    crow = lambda w: pl.BlockSpec((1, past, w), lambda b: (b, 0, 0))
    kern = functools.partial(_dsa_sample_kernel, tq=dseq, past=past, topk=topk)
    return pl.pallas_call(
        kern,
        grid=(dbatch,),
        in_specs=[qrow(HW), qrow(LANES), qrow(LANES), qrow(HW), qrow(HW),
                  crow(HW), crow(HW), crow(IDX_DIM)],
        out_specs=pl.BlockSpec((dseq, HW), lambda b: (b, 0)),
        out_shape=jax.ShapeDtypeStruct((dbatch * dseq, HW), BF16),
        compiler_params=_cparams(("parallel",)),
        name="dsa_sample",
    )(aq, iq, sm, akb, avb, cache_k, cache_v, cache_ik)


def _split3(x):
    hi = x.astype(BF16)
    r1 = x - hi.astype(F32)
    mid = r1.astype(BF16)
    lo = (r1 - mid.astype(F32)).astype(BF16)
    return hi, mid, lo


def _fox_prompt_kernel(qt_ref, k_ref, vt_ref, smk_ref, o_ref, ck3_ref, acc_ref, *, tq, tk, seq):
    j = pl.program_id(1)
    tri = _tri(tk)

    @pl.when(j == 0)
    def _():
        lane = lax.broadcasted_iota(I32, (tk, LANES), 1)
        is_f = (lane >= SM_BF) & (lane < SM_BF + B_HEADS)
        carry = jnp.zeros((1, LANES), F32)
        for r in range(seq // tk):
            lf = jnp.where(is_f, smk_ref[r * tk:(r + 1) * tk, :], 0.0)
            cs = carry
            for piece in _split3(lf):
                cs = cs + _dot(tri, piece)
            carry = cs[tk - 1:tk, :]
            for i, piece in enumerate(_split3(-cs)):
                ck3_ref[r * tk:(r + 1) * tk, i * LANES:(i + 1) * LANES] = piece

    q0 = j * tq
    row = lax.broadcasted_iota(I32, (tk, tq), 0)
    col = lax.broadcasted_iota(I32, (tk, tq), 1)
    qpos = q0 + col
    qh = _split_heads_qt(qt_ref[...])
    srow = lax.broadcasted_iota(I32, (3 * LANES, tq), 0) & (LANES - 1)
    pick = [jnp.where(srow == SM_BF + h, 1.0, 0.0).astype(BF16) for h in range(B_HEADS)]

    def sel_fn(off):
        return (off + row) <= qpos

    def logits_fn(off):
        k_blk = k_ref[pl.ds(off, tk), :]
        ck3 = ck3_ref[pl.ds(off, tk), :]
        return [_dot(k_blk, qh[h]) + _dot(ck3, pick[h]) for h in range(B_HEADS)]

    o_ref[...] = _flash_t(j + 1, tk, tq, logits_fn, sel_fn, vt_ref, acc_ref)


def _fox_prompt(bqt, bkb, bvt, sm, *, batch, seq, tq):
    nq = seq // tq
    tk = tq
    krow = lambda w: pl.BlockSpec((seq, w), lambda b, j: (b, 0))
    kern = functools.partial(_fox_prompt_kernel, tq=tq, tk=tk, seq=seq)
    return pl.pallas_call(
        kern,
        grid=(batch, nq),
        in_specs=[pl.BlockSpec((HW, tq), lambda b, j: (0, b * nq + j)), krow(HW),
                  pl.BlockSpec((HW, seq), lambda b, j: (0, b)), krow(LANES)],
        out_specs=pl.BlockSpec((tq, HW), lambda b, j: (b * nq + j, 0)),
        out_shape=jax.ShapeDtypeStruct((batch * seq, HW), BF16),
        scratch_shapes=[pltpu.VMEM((seq, 3 * LANES), BF16), pltpu.VMEM((HW, tq), F32)],
        compiler_params=_cparams(("parallel", "arbitrary")),
        name="fox_prompt",
    )(bqt, bkb, bvt, sm)


def _fox_sample_kernel(q_ref, kn_ref, vn_ref, ck_ref, cv_ref, clf_ref, nlf_ref, o_ref, *, tq, past):
    pad = LANES - tq
    cum_c = _lane_cumsum(clf_ref[0])
    cum_n = _lane_cumsum(nlf_ref[0])
    bias_c = cum_c[:, past - 1:past] - cum_c
    bias_n = -cum_n
    bias = jnp.concatenate([bias_c, bias_n], axis=1)
    lk = past + LANES
    row = lax.broadcasted_iota(I32, (tq, lk), 0)
    col = lax.broadcasted_iota(I32, (tq, lk), 1)
    sel = col <= past + row
    k_all = jnp.concatenate([ck_ref[0].astype(BF16), kn_ref[...], jnp.zeros((pad, HW), BF16)], axis=0)
    v_all = jnp.concatenate([cv_ref[0].astype(BF16), vn_ref[...], jnp.zeros((pad, HW), BF16)], axis=0)
    qh = _split_heads_q(q_ref[...])
    logits = [_dot_nt(qh[h], k_all) + bias[h:h + 1, :] for h in range(B_HEADS)]
    o_ref[...] = _direct_attend(logits, sel, v_all)


def _fox_sample(bq, bkb, bvb, cache_k, cache_v, cache_lft, new_lft, *, n_prompt, dbatch, dseq):
    past = cache_k.shape[1]
    base = n_prompt // dseq
    qrow = lambda w: pl.BlockSpec((dseq, w), lambda b: (base + b, 0))
    crow = lambda w: pl.BlockSpec((1, past, w), lambda b: (b, 0, 0))
    kern = functools.partial(_fox_sample_kernel, tq=dseq, past=past)
    return pl.pallas_call(
        kern,
        grid=(dbatch,),
        in_specs=[qrow(HW), qrow(HW), qrow(HW), crow(HW), crow(HW),
                  pl.BlockSpec((1, 8, past), lambda b: (b, 0, 0)),
                  pl.BlockSpec((1, 8, LANES), lambda b: (b, 0, 0))],
        out_specs=pl.BlockSpec((dseq, HW), lambda b: (b, 0)),
        out_shape=jax.ShapeDtypeStruct((dbatch * dseq, HW), BF16),
        compiler_params=_cparams(("parallel",)),
        name="fox_sample",
    )(bq, bkb, bvb, cache_k, cache_v, cache_lft, new_lft)


def _mla_logits(q, kfull_blk):
    return [_dot_nt(q[:, h * LANES:(h + 1) * LANES], kfull_blk[:, h * LANES:(h + 1) * LANES])
            for h in range(C_HEADS)]


def _mla_prompt_kernel(qt_ref, ckv_ref, ckvt_ref, smk_ref, w1_ref, w2_ref, w1vt_ref, o_ref,
                       kf_ref, vt_ref, acc_ref, *, tq, tk, seq):
    j = pl.program_id(1)

    @pl.when(j == 0)
    def _():
        rc = 512 if seq % 512 == 0 else tk
        for r in range(seq // rc):
            kf = (_dot(ckv_ref[r * rc:(r + 1) * rc, :].astype(BF16), w1_ref[:, 0:512])
                  + _dot(smk_ref[r * rc:(r + 1) * rc, :].astype(BF16), w2_ref[:, 0:512]))
            kf_ref[r * rc:(r + 1) * rc, :] = kf.astype(BF16)
            vt_ref[:, r * rc:(r + 1) * rc] = _dot(w1vt_ref[...], ckvt_ref[:, r * rc:(r + 1) * rc]).astype(BF16)

    q0 = j * tq
    row = lax.broadcasted_iota(I32, (tk, tq), 0)
    col = lax.broadcasted_iota(I32, (tk, tq), 1)
    qchunk = (q0 + col) >> 6
    qt = qt_ref[...]

    def sel_fn(off):
        return ((off + row) >> 6) <= qchunk

    def logits_fn(off):
        kf = kf_ref[pl.ds(off, tk), :]
        return [_dot(kf[:, h * LANES:(h + 1) * LANES], qt[h * LANES:(h + 1) * LANES, :])
                for h in range(C_HEADS)]

    o_ref[...] = _flash_t(j + 1, tk, tq, logits_fn, sel_fn, vt_ref, acc_ref)


def _mla_prompt(qct, ckv, ckvt, sm, w1, w2, w1vt, *, batch, seq, tq):
    nq = seq // tq
    tk = tq
    krow = lambda w: pl.BlockSpec((seq, w), lambda b, j: (b, 0))
    full = lambda a: pl.BlockSpec(a.shape, lambda b, j: (0,) * a.ndim)
    kern = functools.partial(_mla_prompt_kernel, tq=tq, tk=tk, seq=seq)
    return pl.pallas_call(
        kern,
        grid=(batch, nq),
        in_specs=[pl.BlockSpec((512, tq), lambda b, j: (0, b * nq + j)), krow(LANES),
                  pl.BlockSpec((LANES, seq), lambda b, j: (0, b)), krow(LANES),
                  full(w1), full(w2), full(w1vt)],
        out_specs=pl.BlockSpec((tq, HW), lambda b, j: (b * nq + j, 0)),
        out_shape=jax.ShapeDtypeStruct((batch * seq, HW), BF16),
        scratch_shapes=[pltpu.VMEM((seq, 512), BF16), pltpu.VMEM((HW, seq), BF16),
                        pltpu.VMEM((HW, tq), F32)],
        compiler_params=_cparams(("parallel", "arbitrary")),
        name="mla_prompt",
    )(qct, ckv, ckvt, sm, w1, w2, w1vt)


def _mla_sample_kernel(q_ref, ckvn_ref, smn_ref, cckv_ref, ckr_ref, w1_ref, w2_ref, w2c_ref, o_ref,
                       *, tq, past):
    pad = LANES - tq
    kv_c = _dot(cckv_ref[0].astype(BF16), w1_ref[...]) + _dot(ckr_ref[0].astype(BF16), w2c_ref[...])
    kv_n = _dot(ckvn_ref[...].astype(BF16), w1_ref[...]) + _dot(smn_ref[...].astype(BF16), w2_ref[...])
    kv = jnp.concatenate([kv_c, kv_n, jnp.zeros((pad, 768), F32)], axis=0).astype(BF16)
    lk = past + LANES
    col = lax.broadcasted_iota(I32, (tq, lk), 1)
    sel = col < past + tq
    o_ref[...] = _direct_attend(_mla_logits(q_ref[...], kv[:, :512]), sel, kv[:, 512:])


def _mla_sample(qc, ckv, sm, cache_ckv, cache_kr, w1, w2, w2c, *, n_prompt, dbatch, dseq):
    past = cache_ckv.shape[1]
    base = n_prompt // dseq
    qrow = lambda w: pl.BlockSpec((dseq, w), lambda b: (base + b, 0))
    crow = lambda w: pl.BlockSpec((1, past, w), lambda b: (b, 0, 0))
    full = lambda a: pl.BlockSpec(a.shape, lambda b: (0,) * a.ndim)
    kern = functools.partial(_mla_sample_kernel, tq=dseq, past=past)
    return pl.pallas_call(
        kern,
        grid=(dbatch,),
        in_specs=[qrow(512), qrow(LANES), qrow(LANES), crow(C_KV_RANK), crow(C_ROPE),
                  full(w1), full(w2), full(w2c)],
        out_specs=pl.BlockSpec((dseq, HW), lambda b: (b, 0)),
        out_shape=jax.ShapeDtypeStruct((dbatch * dseq, HW), BF16),
        compiler_params=_cparams(("parallel",)),
        name="mla_sample",
    )(qc, ckv, sm, cache_ckv, cache_kr, w1, w2, w2c)


def _merge_kernel(x_ref, g1_ref, oa_ref, ob_ref, oc_ref, wg_ref, wua_ref, wub_ref, wuc_ref, wout_ref,
                  xo_ref):
    x = x_ref[...]
    d = x.shape[1]
    h = _rmsnorm(x, g1_ref[...]).astype(BF16)
    gates = jax.nn.sigmoid(_dot(h, wg_ref[...]))
    merged = (gates[:, 0:d] * _dot(oa_ref[...], wua_ref[...])
              + gates[:, d:2 * d] * _dot(ob_ref[...], wub_ref[...])
              + gates[:, 2 * d:3 * d] * _dot(oc_ref[...], wuc_ref[...]))
    xo_ref[...] = x + _dot(merged.astype(BF16), wout_ref[...])


def _merge(x, g1, oa, ob, oc, wg, wua, wub, wuc, wout, *, tm):
    n, d = x.shape
    row = lambda w: pl.BlockSpec((tm, w), lambda i: (i, 0))
    full = lambda a: pl.BlockSpec(a.shape, lambda i: (0,) * a.ndim)
    return pl.pallas_call(
        _merge_kernel,
        grid=(n // tm,),
        in_specs=[row(d), full(g1), row(HW), row(HW), row(HW),
                  full(wg), full(wua), full(wub), full(wuc), full(wout)],
        out_specs=row(d),
        out_shape=jax.ShapeDtypeStruct((n, d), F32),
        compiler_params=_cparams(("parallel",)),
        name="merge",
    )(x, g1, oa, ob, oc, wg, wua, wub, wuc, wout)


def _top_values(s, k):
    vals = []
    for _ in range(k):
        m = jnp.max(s, axis=0, keepdims=True)
        vals.append(m)
        s = jnp.where(s == m, -jnp.inf, s)
    return vals


def _peer_kernel(x_ref, g2_ref, wqt_ref, k1_ref, k2_ref, u_ref, vt_ref, o_ref,
                 ht_ref, acc_ref, s2_ref, e2_ref, th_ref, c_ref, v2_ref, cand_ref, g_ref,
                 *, nk, na, ncand):
    k = pl.program_id(1)
    tm = x_ref.shape[0]
    half = PEER_QDIM // 2

    @pl.when(k == 0)
    def _():
        h = _rmsnorm(x_ref[...], g2_ref[...])
        ht = h.T.astype(BF16)
        ht_ref[...] = ht
        qt = _dot(wqt_ref[...], ht)
        acc_ref[...] = jnp.zeros(acc_ref.shape, F32)
        for hh in range(PEER_HEADS):
            q1 = qt[hh * PEER_QDIM:hh * PEER_QDIM + half, :].astype(BF16)
            q2 = qt[hh * PEER_QDIM + half:(hh + 1) * PEER_QDIM, :].astype(BF16)
            s1 = _dot(k1_ref[hh * nk:(hh + 1) * nk, :], q1)
            s2 = _dot(k2_ref[hh * nk:(hh + 1) * nk, :], q2)
            v1 = _top_values(s1, PEER_TOPK)
            v2 = _top_values(s2, PEER_TOPK)
            for jj in range(PEER_TOPK):
                v2_ref[jj:jj + 1, :] = v2[jj]
            cand_ref[...] = jnp.full(cand_ref.shape, -jnp.inf, F32)
            off = 0
            for i in range(PEER_TOPK):
                nj = PEER_TOPK // (i + 1)
                cand_ref[off:off + nj, :] = v1[i] + v2_ref[0:nj, :]
                off += nj
            cand = cand_ref[...]
            tau = _top_values(cand, PEER_TOPK)[-1]
            top = v1[0] + v2[0]
            z = jnp.sum(jnp.where(cand >= tau, jnp.exp(cand - top), 0.0), axis=0, keepdims=True)
            th = jnp.full((nk, tm), jnp.inf, F32)
            for jj in range(PEER_TOPK):
                th = jnp.where((s1 + v2[jj]) >= tau, v2[jj], th)
            s2_ref[hh] = s2
            e2_ref[hh] = jnp.exp(s2 - v2[0])
            th_ref[:, hh, :] = th
            c_ref[:, hh, :] = jnp.exp(s1 - v1[0]) / z

    st = _dot(u_ref[...], ht_ref[...])
    act = jax.nn.gelu(st)
    a0 = pl.multiple_of(k * na, na)
    th = th_ref[pl.ds(a0, na)]
    cc = c_ref[pl.ds(a0, na)]
    for ai in range(na):
        w = None
        for hh in range(PEER_HEADS):
            t = jnp.where(s2_ref[hh] >= th[ai, hh:hh + 1, :], e2_ref[hh], 0.0) * cc[ai, hh:hh + 1, :]
            w = t if w is None else w + t
        g_ref[ai * nk:(ai + 1) * nk, :] = (w * act[ai * nk:(ai + 1) * nk, :]).astype(BF16)
    acc_ref[...] += _dot(vt_ref[...], g_ref[...])

    @pl.when(k == pl.num_programs(1) - 1)
    def _():
        o_ref[...] = x_ref[...] + acc_ref[...].T


def _peer(x, g2, wqt, k1, k2, u, vt, *, tm, na):
    n, d = x.shape
    nk = k1.shape[0] // PEER_HEADS
    te = na * nk
    nchunks = nk // na
    ncand = sum(PEER_TOPK // (i + 1) for i in range(PEER_TOPK))
    ncand = -(-ncand // 8) * 8
    full = lambda a: pl.BlockSpec(a.shape, lambda i, k: (0,) * a.ndim)
    kern = functools.partial(_peer_kernel, nk=nk, na=na, ncand=ncand)
    return pl.pallas_call(
        kern,
        grid=(n // tm, nchunks),
        in_specs=[pl.BlockSpec((tm, d), lambda i, k: (i, 0)), full(g2), full(wqt), full(k1), full(k2),
                  pl.BlockSpec((te, d), lambda i, k: (k, 0)),
                  pl.BlockSpec((d, te), lambda i, k: (0, k))],
        out_specs=pl.BlockSpec((tm, d), lambda i, k: (i, 0)),
        out_shape=jax.ShapeDtypeStruct((n, d), F32),
        scratch_shapes=[pltpu.VMEM((d, tm), BF16), pltpu.VMEM((d, tm), F32),
                        pltpu.VMEM((PEER_HEADS, nk, tm), F32), pltpu.VMEM((PEER_HEADS, nk, tm), F32),
                        pltpu.VMEM((nk, PEER_HEADS, tm), F32), pltpu.VMEM((nk, PEER_HEADS, tm), F32),
                        pltpu.VMEM((PEER_TOPK, tm), F32), pltpu.VMEM((ncand, tm), F32),
                        pltpu.VMEM((te, tm), BF16)],
        compiler_params=_cparams(("parallel", "arbitrary")),
        name="peer",
    )(x, g2, wqt, k1, k2, u, vt)


def _final_norm_kernel(x_ref, g_ref, o_ref):
    o_ref[...] = _rmsnorm(x_ref[...], g_ref[...])


def _final_norm(x, g, *, tm):
    n, d = x.shape
    return pl.pallas_call(
        _final_norm_kernel,
        grid=(n // tm,),
        in_specs=[pl.BlockSpec((tm, d), lambda i: (i, 0)), pl.BlockSpec((1, d), lambda i: (0, 0))],
        out_specs=pl.BlockSpec((tm, d), lambda i: (i, 0)),
        out_shape=jax.ShapeDtypeStruct((n, d), F32),
        compiler_params=_cparams(("parallel",)),
        name="final_norm",
    )(x, g)


def _rope_tables(pos):
    rows = pos.shape[0]

    def cs(half):
        inv = jnp.power(ROPE_THETA, -jnp.arange(half, dtype=F32) / half)
        ang = pos.astype(F32)[:, None] * inv[None, :]
        c, s = jnp.cos(ang), jnp.sin(ang)
        return jnp.concatenate([c, c], axis=1), jnp.concatenate([-s, s], axis=1)

    c64, s64 = cs(HEAD_DIM // 2)
    c32, s32 = cs(IDX_DIM // 2)
    one32 = jnp.ones((rows, 32), F32)
    zero32 = jnp.zeros((rows, 32), F32)
    c_sm = jnp.concatenate([c32, c32, one32, one32], axis=1)
    s_sm = jnp.concatenate([s32, s32, zero32, zero32], axis=1)
    c_qc = jnp.concatenate([one32, one32, c32, one32], axis=1)
    s_qc = jnp.concatenate([zero32, zero32, s32, zero32], axis=1)
    ctab = jnp.concatenate([jnp.tile(c64, (1, 4)), jnp.tile(c32, (1, 4)), c_sm, jnp.tile(c_qc, (1, 4))], axis=1)
    stab = jnp.concatenate([jnp.tile(s64, (1, 4)), jnp.tile(s32, (1, 4)), s_sm, jnp.tile(s_qc, (1, 4))], axis=1)
    return ctab, stab


def _split_w_in(w_in):
    sizes = (HW, HW, HW, IDX_HEADS * IDX_DIM, IDX_DIM, IDX_HEADS, HW, HW, HW, B_HEADS,
             C_Q_RANK, C_KV_RANK, C_ROPE)
    offs = np.cumsum((0,) + sizes)
    parts = [w_in[..., int(offs[i]):int(offs[i + 1])] for i in range(len(sizes))]
    gates = w_in[..., int(offs[-1]):]
    return parts, gates


def _pack_weights(w_in, w_uq, w_ukv, b_forget):
    depth, d, _ = w_in.shape
    (a_q, a_k, a_v, i_q, i_k, i_w, b_q, b_k, b_v, b_f, c_q, c_kv, c_kr), gates = _split_w_in(w_in)
    z = lambda w: jnp.zeros((depth, d, w), F32)
    small = jnp.concatenate([i_k, c_kr, b_f, z(4), i_w, z(LANES - SM_IW - IDX_HEADS)], axis=-1)
    wmain = jnp.concatenate([a_q, a_k, a_v, b_q, b_k, b_v, c_q, c_kv, i_q, small], axis=-1).astype(BF16)
    wg = gates.astype(BF16)

    uq = w_uq.reshape(depth, C_Q_RANK, C_HEADS, C_NOPE + C_ROPE)
    uq = jnp.concatenate([uq, jnp.zeros((depth, C_Q_RANK, C_HEADS, 32), F32)], axis=-1)
    wuq = uq.reshape(depth, C_Q_RANK, C_HEADS * LANES).astype(BF16)

    ukv = w_ukv.reshape(depth, C_KV_RANK, C_HEADS, C_NOPE + C_VDIM)
    kn = jnp.concatenate([ukv[..., :C_NOPE], jnp.zeros((depth, C_KV_RANK, C_HEADS, 64), F32)], axis=-1)
    w1 = jnp.concatenate([kn.reshape(depth, C_KV_RANK, 512),
                          ukv[..., C_NOPE:].reshape(depth, C_KV_RANK, HW)], axis=-1).astype(BF16)
    w1vt = jnp.swapaxes(w1[:, :, 512:], 1, 2)
    eye = np.zeros((C_ROPE, 768), np.float32)
    for h in range(C_HEADS):
        for r in range(C_ROPE):
            eye[r, h * LANES + C_NOPE + r] = 1.0
    w2c = jnp.asarray(eye).astype(BF16)
    w2_np = np.zeros((LANES, 768), np.float32)
    w2_np[SM_KR:SM_KR + C_ROPE] = eye
    w2 = jnp.asarray(w2_np).astype(BF16)

    aux = jnp.zeros((depth, 8, LANES), F32)
    aux = aux.at[:, 0, SM_BF:SM_BF + B_HEADS].set(b_forget)
    scale = np.ones((LANES,), np.float32)
    scale[SM_IW:SM_IW + IDX_HEADS] = (IDX_HEADS * IDX_DIM) ** -0.5
    aux = aux.at[:, 1, :].set(jnp.asarray(scale))
    return wmain, wg, wuq, w1, w1vt, w2, w2c, aux


def _tile_rows(n):
    for t in (512, 256, 128):
        if n % t == 0:
            return t
    raise ValueError(f"token count {n} is not a multiple of 128")


def kernel(x_prompt, x_sample, cache_k_a, cache_v_a, cache_kidx_a, cache_k_b, cache_v_b, cache_logf_b,
           cache_ckv_c, cache_krope_c, norm1_g, w_in, c_q_norm_g, c_kv_norm_g, w_uq, w_ukv, b_forget,
           w_up_a, w_up_b, w_up_c, w_out, norm2_g, peer_w_q, peer_keys, peer_u, peer_v, final_norm_g):
    batch, seq, d = x_prompt.shape
    dbatch, dseq, _ = x_sample.shape
    depth = w_in.shape[0]
    past = cache_k_a.shape[2]
    n_p = batch * seq
    n_s = dbatch * dseq
    n = n_p + n_s
    nk = peer_keys.shape[3]
    assert dseq == CHUNK and past % LANES == 0 and seq % 256 == 0

    tm = min(_tile_rows(seq), _tile_rows(n_s))
    tq = 256
    topk_p = min(TOPK_MAX, seq // 4)
    topk_s = min(TOPK_MAX, (past + dseq) // 4)

    pos = jnp.concatenate([jnp.arange(seq, dtype=jnp.int32),
                           past + (jnp.arange(tm, dtype=jnp.int32) % dseq)])
    ctab, stab = _rope_tables(pos)

    wmain, wg, wuq, w1, w1vt, w2, w2c, aux = _pack_weights(w_in, w_uq, w_ukv, b_forget)
    wua, wub, wuc, wout = (w.astype(BF16) for w in (w_up_a, w_up_b, w_up_c, w_out))
    wqt = jnp.swapaxes(peer_w_q, 1, 2).astype(BF16)
    half = PEER_QDIM // 2
    k1 = peer_keys[:, :, 0].reshape(depth, PEER_HEADS * nk, half).astype(BF16)
    k2 = peer_keys[:, :, 1].reshape(depth, PEER_HEADS * nk, half).astype(BF16)
    pu = peer_u.astype(BF16)
    pvt = jnp.swapaxes(peer_v, 1, 2).astype(BF16)
    clf_t = jnp.swapaxes(cache_logf_b, 2, 3)
    clf_t = jnp.concatenate([clf_t, jnp.zeros_like(clf_t)], axis=2)

    g_row = lambda g: g.reshape(1, -1)
    x = jnp.concatenate([x_prompt.reshape(n_p, d), x_sample.reshape(n_s, d)], axis=0)
    rows_p = [[] for _ in range(8)]
    rows_s = [[] for _ in range(8)]
    na = 8 if nk % 8 == 0 else nk

    for l in range(depth):
        g1 = g_row(norm1_g[l])
        (aq, akf, akb, avf, avb, bq, bkf, bkb, bvf, bvb, qc, ckv, iq, sm, lft,
         aqt, avt, bqt, bvt, qct, iqt, ckvt, iwt) = _inproj(
            x, g1, wmain[l], wuq[l], g_row(c_q_norm_g[l]), g_row(c_kv_norm_g[l]), aux[l], ctab, stab,
            tm=tm, n_prompt=n_p, seq=seq)

        new_lft = lft[:, n_p:].reshape(8, dbatch, dseq).transpose(1, 0, 2)
        new_lft = jnp.concatenate([new_lft, jnp.zeros((dbatch, 8, LANES - dseq), F32)], axis=2)

        oa_p = _dsa_prompt(aqt, iqt, iwt, akb, avt, sm, batch=batch, seq=seq, tq=tq, topk=topk_p)
        oa_s = _dsa_sample(aq, iq, sm, akb, avb,
                           cache_k_a[l].reshape(dbatch, past, HW), cache_v_a[l].reshape(dbatch, past, HW),
                           cache_kidx_a[l], n_prompt=n_p, dbatch=dbatch, dseq=dseq, topk=topk_s)
        ob_p = _fox_prompt(bqt, bkb, bvt, sm, batch=batch, seq=seq, tq=tq)
        ob_s = _fox_sample(bq, bkb, bvb,
                           cache_k_b[l].reshape(dbatch, past, HW), cache_v_b[l].reshape(dbatch, past, HW),
                           clf_t[l], new_lft, n_prompt=n_p, dbatch=dbatch, dseq=dseq)
        oc_p = _mla_prompt(qct, ckv, ckvt, sm, w1[l], w2, w1vt[l], batch=batch, seq=seq, tq=tq)
        oc_s = _mla_sample(qc, ckv, sm, cache_ckv_c[l], cache_krope_c[l], w1[l], w2, w2c,
                           n_prompt=n_p, dbatch=dbatch, dseq=dseq)
        oa = jnp.concatenate([oa_p, oa_s], axis=0)
        ob = jnp.concatenate([ob_p, ob_s], axis=0)
        oc = jnp.concatenate([oc_p, oc_s], axis=0)

        x = _merge(x, g1, oa, ob, oc, wg[l], wua[l], wub[l], wuc[l], wout[l], tm=tm)
        x = _peer(x, g_row(norm2_g[l]), wqt[l], k1[l], k2[l], pu[l], pvt[l], tm=tm, na=na)

        new = (akf, avf, sm[:, SM_IK:SM_IK + IDX_DIM], bkf, bvf, sm[:, SM_BF:SM_BF + B_HEADS],
               ckv, sm[:, SM_KR:SM_KR + C_ROPE])
        for i, a in enumerate(new):
            rows_p[i].append(a[:n_p])
            rows_s[i].append(a[n_p:])

    y = _final_norm(x, g_row(final_norm_g), tm=tm)
    y_p = y[:n_p].reshape(batch, seq, d)
    y_s = y[n_p:].reshape(dbatch, dseq, d)

    tails = [(A_HEADS, HEAD_DIM), (A_HEADS, HEAD_DIM), (IDX_DIM,), (B_HEADS, HEAD_DIM), (B_HEADS, HEAD_DIM),
             (B_HEADS,), (C_KV_RANK,), (C_ROPE,)]
    outs_p = [jnp.stack(r).reshape((depth, batch, seq) + t) for r, t in zip(rows_p, tails)]
    outs_s = [jnp.stack(r).reshape((depth, dbatch, dseq) + t) for r, t in zip(rows_s, tails)]
    return (y_p, y_s, *outs_p, *outs_s)
```

```python
import functools

import jax
import jax.numpy as jnp
import numpy as np
from jax import lax
from jax.experimental import pallas as pl
from jax.experimental.pallas import tpu as pltpu

F32 = jnp.float32
BF16 = jnp.bfloat16
I32 = jnp.int32

CHUNK = 64
HEAD_DIM = 64
ROPE_THETA = 10000.0
NORM_EPS = 1e-6
A_HEADS = 4
IDX_HEADS = 4
IDX_DIM = 32
TOPK_MAX = 256
B_HEADS = 4
C_HEADS = 4
C_NOPE = 64
C_ROPE = 32
C_VDIM = 64
C_Q_RANK = 256
C_KV_RANK = 128
N_BRANCHES = 3
PEER_HEADS = 8
PEER_QDIM = 128
PEER_TOPK = 16

HW = A_HEADS * HEAD_DIM
LANES = 128
VMEM_LIMIT = 56 * 1024 * 1024

SM_IK = 0
SM_KR = 32
SM_BF = 64
SM_IW = 72

MC_AQ, MC_AK, MC_AV = 0, 256, 512
MC_BQ, MC_BK, MC_BV = 768, 1024, 1280
MC_CQ = 1536
MC_CKV = 1792
MC_IQ = 1920
MC_SM = 2048
MAIN_COLS = 2176

RT_64 = 0
RT_IQ = 256
RT_SM = 384
RT_QC = 512
RT_COLS = 1024

NEG = -1e30
KEY_NEG_INF = -2139095041
INT_MIN = -2147483648


def _cparams(sem):
    return pltpu.CompilerParams(dimension_semantics=sem, vmem_limit_bytes=VMEM_LIMIT)


def _rmsnorm(x, g):
    ms = jnp.mean(x * x, axis=-1, keepdims=True)
    return x * lax.rsqrt(ms + NORM_EPS) * g


def _dot(a, b):
    return jnp.dot(a, b, preferred_element_type=F32)


def _dot_nt(a, b):
    return lax.dot_general(a, b, (((1,), (1,)), ((), ())), preferred_element_type=F32)


def _rope(v, c, s, half):
    w = v.shape[-1]
    lane = lax.broadcasted_iota(I32, v.shape, 1)
    first = (lane & (2 * half - 1)) < half
    vr = jnp.where(first, pltpu.roll(v, w - half, 1), pltpu.roll(v, half, 1))
    return v * c + vr * s


def _inproj_kernel(x_ref, g1_ref, wmain_ref, wuq_ref, gq_ref, gkv_ref, aux_ref, ct_ref, st_ref,
                   aq_ref, akf_ref, akb_ref, avf_ref, avb_ref,
                   bq_ref, bkf_ref, bkb_ref, bvf_ref, bvb_ref,
                   qc_ref, ckv_ref, iq_ref, sm_ref, lft_ref,
                   aqt_ref, avt_ref, bqt_ref, bvt_ref, qct_ref, iqt_ref, ckvt_ref, iwt_ref):
    x = x_ref[...]
    h = _rmsnorm(x, g1_ref[...]).astype(BF16)
    y = _dot(h, wmain_ref[...])
    ct = ct_ref[...]
    st = st_ref[...]

    c64 = ct[:, RT_64:RT_64 + HW]
    s64 = st[:, RT_64:RT_64 + HW]
    aq = _rope(y[:, MC_AQ:MC_AQ + HW], c64, s64, HEAD_DIM // 2)
    aq = aq * (HEAD_DIM ** -0.5)
    aq_ref[...] = aq.astype(BF16)
    aqt_ref[...] = aq.T.astype(BF16)
    ak = _rope(y[:, MC_AK:MC_AK + HW], c64, s64, HEAD_DIM // 2)
    akf_ref[...] = ak
    akb_ref[...] = ak.astype(BF16)
    av = y[:, MC_AV:MC_AV + HW]
    avf_ref[...] = av
    avb_ref[...] = av.astype(BF16)
    avt_ref[...] = av.T.astype(BF16)

    bq = y[:, MC_BQ:MC_BQ + HW] * (HEAD_DIM ** -0.5)
    bq_ref[...] = bq.astype(BF16)
    bqt_ref[...] = bq.T.astype(BF16)
    bk = y[:, MC_BK:MC_BK + HW]
    bkf_ref[...] = bk
    bkb_ref[...] = bk.astype(BF16)
    bv = y[:, MC_BV:MC_BV + HW]
    bvf_ref[...] = bv
    bvb_ref[...] = bv.astype(BF16)
    bvt_ref[...] = bv.T.astype(BF16)

    cq = _rmsnorm(y[:, MC_CQ:MC_CQ + C_Q_RANK], gq_ref[...]).astype(BF16)
    qc = _dot(cq, wuq_ref[...])
    qc = _rope(qc, ct[:, RT_QC:RT_QC + 512], st[:, RT_QC:RT_QC + 512], C_ROPE // 2)
    qc = qc * ((C_NOPE + C_ROPE) ** -0.5)
    qc_ref[...] = qc.astype(BF16)
    qct_ref[...] = qc.T.astype(BF16)

    ckv = _rmsnorm(y[:, MC_CKV:MC_CKV + C_KV_RANK], gkv_ref[...])
    ckv_ref[...] = ckv
    ckvt_ref[...] = ckv.T.astype(BF16)

    iq = _rope(y[:, MC_IQ:MC_IQ + LANES], ct[:, RT_IQ:RT_IQ + LANES],
               st[:, RT_IQ:RT_IQ + LANES], IDX_DIM // 2)
    iq_ref[...] = iq
    iqt_ref[...] = iq.T.astype(BF16)

    sm = _rope(y[:, MC_SM:MC_SM + LANES], ct[:, RT_SM:RT_SM + LANES],
               st[:, RT_SM:RT_SM + LANES], IDX_DIM // 2)
    lane = lax.broadcasted_iota(I32, sm.shape, 1)
    is_f = (lane >= SM_BF) & (lane < SM_BF + B_HEADS)
    z = sm + aux_ref[0:1, :]
    logsig = jnp.minimum(z, 0.0) - jnp.log1p(jnp.exp(-jnp.abs(z)))
    sm = jnp.where(is_f, logsig, sm * aux_ref[1:2, :])
    sm_ref[...] = sm
    smt = sm.T
    lft_ref[...] = smt[SM_BF:SM_BF + 8, :]
    iwt_ref[...] = smt[SM_IW:SM_IW + 8, :]


def _inproj(x, g1, wmain, wuq, gq, gkv, aux, ctab, stab, *, tm, n_prompt, seq):
    n, d = x.shape
    nt = n // tm
    npt = n_prompt // tm
    tps = seq // tm

    def tmap(i):
        return (jnp.where(i < npt, i % tps, tps), 0)

    row = lambda w: pl.BlockSpec((tm, w), lambda i: (i, 0))
    full = lambda a: pl.BlockSpec(a.shape, lambda i: (0,) * a.ndim)
    out_shapes = [
        jax.ShapeDtypeStruct((n, HW), BF16),
        jax.ShapeDtypeStruct((n, HW), F32),
        jax.ShapeDtypeStruct((n, HW), BF16),
        jax.ShapeDtypeStruct((n, HW), F32),
        jax.ShapeDtypeStruct((n, HW), BF16),
        jax.ShapeDtypeStruct((n, HW), BF16),
        jax.ShapeDtypeStruct((n, HW), F32),
        jax.ShapeDtypeStruct((n, HW), BF16),
        jax.ShapeDtypeStruct((n, HW), F32),
        jax.ShapeDtypeStruct((n, HW), BF16),
        jax.ShapeDtypeStruct((n, 512), BF16),
        jax.ShapeDtypeStruct((n, LANES), F32),
        jax.ShapeDtypeStruct((n, LANES), F32),
        jax.ShapeDtypeStruct((n, LANES), F32),
        jax.ShapeDtypeStruct((8, n), F32),
        jax.ShapeDtypeStruct((HW, n), BF16),
        jax.ShapeDtypeStruct((HW, n), BF16),
        jax.ShapeDtypeStruct((HW, n), BF16),
        jax.ShapeDtypeStruct((HW, n), BF16),
        jax.ShapeDtypeStruct((512, n), BF16),
        jax.ShapeDtypeStruct((LANES, n), BF16),
        jax.ShapeDtypeStruct((LANES, n), BF16),
        jax.ShapeDtypeStruct((8, n), F32),
    ]
    col = lambda w: pl.BlockSpec((w, tm), lambda i: (0, i))
    out_specs = [row(HW)] * 10 + [row(512), row(LANES), row(LANES), row(LANES), col(8),
                                  col(HW), col(HW), col(HW), col(HW), col(512), col(LANES), col(LANES),
                                  col(8)]
    return pl.pallas_call(
        _inproj_kernel,
        grid=(nt,),
        in_specs=[row(d), full(g1), full(wmain), full(wuq), full(gq), full(gkv), full(aux),
                  pl.BlockSpec((tm, RT_COLS), tmap), pl.BlockSpec((tm, RT_COLS), tmap)],
        out_specs=out_specs,
        out_shape=out_shapes,
        compiler_params=_cparams(("parallel",)),
        name="inproj",
    )(x, g1, wmain, wuq, gq, gkv, aux, ctab, stab)


def _head_lane_mask(shape, h, width=HEAD_DIM):
    lane = lax.broadcasted_iota(I32, shape, 1)
    return (lane >= h * width) & (lane < (h + 1) * width)


def _split_heads_q(q):
    return [jnp.where(_head_lane_mask(q.shape, h), q, jnp.zeros_like(q)) for h in range(4)]


def _flash_init(m_ref, l_ref, acc_ref):
    m_ref[...] = jnp.full(m_ref.shape, NEG, F32)
    l_ref[...] = jnp.zeros(l_ref.shape, F32)
    acc_ref[...] = jnp.zeros(acc_ref.shape, F32)


def _flash_update(h, logits, sel, v_blk, m_ref, l_ref, acc_ref):
    lm = jnp.where(sel, logits, NEG)
    m_old = m_ref[h]
    m_new = jnp.maximum(m_old, jnp.max(lm, axis=1, keepdims=True))
    p = jnp.where(sel, jnp.exp(logits - m_new), 0.0)
    alpha = jnp.exp(m_old - m_new)
    l_ref[h] = alpha * l_ref[h] + jnp.sum(p, axis=1, keepdims=True)
    acc_ref[h] = alpha * acc_ref[h] + _dot(p.astype(BF16), v_blk)
    m_ref[h] = m_new


def _flash_finish(l_ref, acc_ref):
    out = None
    for h in range(4):
        o = acc_ref[h] / l_ref[h]
        o = jnp.where(_head_lane_mask(o.shape, h), o, 0.0)
        out = o if out is None else out + o
    return out.astype(BF16)


def _direct_attend(logits_list, sel, v_blk):
    out = None
    for h, logits in enumerate(logits_list):
        lm = jnp.where(sel, logits, NEG)
        m = jnp.max(lm, axis=1, keepdims=True)
        p = jnp.where(sel, jnp.exp(logits - m), 0.0)
        l = jnp.sum(p, axis=1, keepdims=True)
        o = _dot(p.astype(BF16), v_blk) / l
        o = jnp.where(_head_lane_mask(o.shape, h), o, 0.0)
        out = o if out is None else out + o
    return out.astype(BF16)


def _lane_cumsum(x):
    n = x.shape[1]
    lane = lax.broadcasted_iota(I32, x.shape, 1)
    k = 1
    while k < n:
        x = x + jnp.where(lane >= k, pltpu.roll(x, k, 1), 0.0)
        k *= 2
    return x


def _key_to_float(key):
    bits = key ^ ((key >> 31) & 0x7FFFFFFF)
    return jnp.where(key < KEY_NEG_INF, -jnp.inf, pltpu.bitcast(bits, F32))


def _kth_largest(count_ge, shape, k):
    zero = jnp.zeros(shape, I32)
    c0 = count_ge(_key_to_float(zero))
    lo = jnp.where(c0 >= k, zero, jnp.full(shape, INT_MIN, I32))

    def bit_body(i, lo):
        trial = lo + jnp.left_shift(jnp.int32(1), 30 - i)
        c = count_ge(_key_to_float(trial))
        return jnp.where(c >= k, trial, lo)

    lo = lax.fori_loop(0, 31, bit_body, lo)
    return _key_to_float(lo)


def _tri128():
    r = lax.broadcasted_iota(I32, (LANES, LANES), 0)
    c = lax.broadcasted_iota(I32, (LANES, LANES), 1)
    return jnp.where(r <= c, 1.0, 0.0).astype(BF16)


def _tri(n):
    r = lax.broadcasted_iota(I32, (n, n), 0)
    c = lax.broadcasted_iota(I32, (n, n), 1)
    return jnp.where(c <= r, 1.0, 0.0).astype(BF16)


def _split_heads_qt(qt):
    row = lax.broadcasted_iota(I32, qt.shape, 0)
    return [jnp.where((row >= h * HEAD_DIM) & (row < (h + 1) * HEAD_DIM), qt, jnp.zeros_like(qt))
            for h in range(4)]


def _flash_t(nkb, tk, tq, logits_fn, sel_fn, vt_ref, acc_ref):
    acc_ref[...] = jnp.zeros(acc_ref.shape, F32)

    def body(kb, carry):
        ms, ls = carry
        off = pl.multiple_of(kb * tk, tk)
        sel = sel_fn(off)
        lgs = logits_fn(off)
        new_m, new_l = [], []
        for h in range(4):
            lm = jnp.where(sel, lgs[h], NEG)
            m_new = jnp.maximum(ms[h], jnp.max(lm, axis=0, keepdims=True))
            p = jnp.exp(lm - m_new)
            alpha = jnp.exp(ms[h] - m_new)
            new_l.append(alpha * ls[h] + jnp.sum(p, axis=0, keepdims=True))
            rows = slice(h * HEAD_DIM, (h + 1) * HEAD_DIM)
            acc_ref[rows, :] = alpha * acc_ref[rows, :] + _dot(vt_ref[rows, pl.ds(off, tk)], p.astype(BF16))
            new_m.append(m_new)
        return tuple(new_m), tuple(new_l)

    init = (tuple(jnp.full((1, tq), NEG / 2, F32) for _ in range(4)),
            tuple(jnp.zeros((1, tq), F32) for _ in range(4)))
    _, ls = lax.fori_loop(0, nkb, body, init)
    out = jnp.concatenate([acc_ref[h * HEAD_DIM:(h + 1) * HEAD_DIM, :] / ls[h] for h in range(4)], axis=0)
    return out.T.astype(BF16)


def _tie_select(eq, rem, tri):
    parts = []
    for c in range(eq.shape[1] // LANES):
        e = eq[:, c * LANES:(c + 1) * LANES]
        pre = _dot(jnp.where(e, 1.0, 0.0).astype(BF16), tri)
        parts.append(jnp.where(e & (pre <= rem), 1, 0))
        rem = rem - pre[:, LANES - 1:LANES]
    return jnp.concatenate(parts, axis=1), rem


def _dsa_query_prep(iq, smq):
    lane = lax.broadcasted_iota(I32, iq.shape, 1)
    iqh = []
    for h in range(IDX_HEADS):
        r = iq if h == 0 else pltpu.roll(iq, LANES - IDX_DIM * h, 1)
        iqh.append(jnp.where(lane < IDX_DIM, r, 0.0).astype(BF16))
    wcol = [smq[:, SM_IW + h:SM_IW + h + 1] for h in range(IDX_HEADS)]
    return iqh, wcol


def _dsa_scores(iqh, wcol, ik_blk, contract32=False):
    sc = None
    for h in range(IDX_HEADS):
        a = iqh[h][:, :IDX_DIM] if contract32 else iqh[h]
        z = _dot_nt(a, ik_blk)
        t = jnp.maximum(z, 0.0) * wcol[h]
        sc = t if sc is None else sc + t
    return sc


def _dsa_prompt_kernel(qt_ref, iqt_ref, iwt_ref, k_ref, vt_ref, smk_ref, o_ref,
                       sc_ref, acc_ref, rem_ref, *, tq, tk, topk):
    j = pl.program_id(1)
    nkb = j + 1
    q0 = j * tq
    iqt = iqt_ref[...]
    pad = jnp.zeros((LANES - IDX_DIM, tq), BF16)
    iqh = [jnp.concatenate([iqt[IDX_DIM * h:IDX_DIM * (h + 1), :], pad], axis=0) for h in range(IDX_HEADS)]
    iw = iwt_ref[...]
    row = lax.broadcasted_iota(I32, (tk, tq), 0)
    col = lax.broadcasted_iota(I32, (tk, tq), 1)
    qchunk = (q0 + col) >> 6

    def visible(off):
        return ((off + row) >> 6) <= qchunk

    def score_body(kb, carry):
        off = pl.multiple_of(kb * tk, tk)
        ik = smk_ref[pl.ds(off, tk), :].astype(BF16)
        sc = None
        for h in range(IDX_HEADS):
            t = jnp.maximum(_dot(ik, iqh[h]), 0.0) * iw[h:h + 1, :]
            sc = t if sc is None else sc + t
        sc_ref[pl.ds(off, tk), :] = jnp.where(visible(off), sc, -jnp.inf)
        return carry

    lax.fori_loop(0, nkb, score_body, 0)

    def count_cmp(trial, strict):
        def body(kb, c):
            off = pl.multiple_of(kb * tk, tk)
            s = sc_ref[pl.ds(off, tk), :]
            hit = (s > trial) if strict else (s >= trial)
            return c + jnp.sum(jnp.where(hit, 1, 0), axis=0, keepdims=True)
        return lax.fori_loop(0, nkb, body, jnp.zeros((1, tq), I32))

    thr = _kth_largest(lambda t: count_cmp(t, False), (1, tq), topk)
    rem_ref[0:1, :] = (topk - count_cmp(thr, True)).astype(F32)
    qh = _split_heads_qt(qt_ref[...])
    tri = _tri(tk)

    def sel_fn(off):
        s = sc_ref[pl.ds(off, tk), :]
        eq = s == thr
        pre = _dot(tri, jnp.where(eq, 1.0, 0.0).astype(BF16))
        rem = rem_ref[0:1, :]
        keep = eq & (pre <= rem)
        rem_ref[0:1, :] = rem - pre[tk - 1:tk, :]
        return visible(off) & ((s > thr) | keep)

    def logits_fn(off):
        k_blk = k_ref[pl.ds(off, tk), :]
        return [_dot(k_blk, qh[h]) for h in range(A_HEADS)]

    o_ref[...] = _flash_t(nkb, tk, tq, logits_fn, sel_fn, vt_ref, acc_ref)


def _dsa_prompt(aqt, iqt, iwt, akb, avt, sm, *, batch, seq, tq, topk):
    nq = seq // tq
    tk = tq
    qcol = lambda w: pl.BlockSpec((w, tq), lambda b, j: (0, b * nq + j))
    krow = lambda w: pl.BlockSpec((seq, w), lambda b, j: (b, 0))
    kern = functools.partial(_dsa_prompt_kernel, tq=tq, tk=tk, topk=topk)
    return pl.pallas_call(
        kern,
        grid=(batch, nq),
        in_specs=[qcol(HW), qcol(LANES), qcol(8), krow(HW),
                  pl.BlockSpec((HW, seq), lambda b, j: (0, b)), krow(LANES)],
        out_specs=pl.BlockSpec((tq, HW), lambda b, j: (b * nq + j, 0)),
        out_shape=jax.ShapeDtypeStruct((batch * seq, HW), BF16),
        scratch_shapes=[pltpu.VMEM((seq, tq), F32), pltpu.VMEM((HW, tq), F32), pltpu.VMEM((8, tq), F32)],
        compiler_params=_cparams(("parallel", "arbitrary")),
        name="dsa_prompt",
    )(aqt, iqt, iwt, akb, avt, sm)


def _dsa_sample_kernel(q_ref, iq_ref, smq_ref, kn_ref, vn_ref, ck_ref, cv_ref, cik_ref, o_ref,
                       *, tq, past, topk):
    pad = LANES - tq
    iqh, wcol = _dsa_query_prep(iq_ref[...], smq_ref[...])
    sc_c = _dsa_scores(iqh, wcol, cik_ref[0].astype(BF16), contract32=True)
    ik_new = jnp.concatenate([smq_ref[...].astype(BF16), jnp.zeros((pad, LANES), BF16)], axis=0)
    sc_n = _dsa_scores(iqh, wcol, ik_new)
    lane = lax.broadcasted_iota(I32, (tq, LANES), 1)
    sc_n = jnp.where(lane < tq, sc_n, -jnp.inf)
    sc = jnp.concatenate([sc_c, sc_n], axis=1)
    lk = past + LANES
    col = lax.broadcasted_iota(I32, (tq, lk), 1)
    vis = col < past + tq

    def count_ge(trial):
        return jnp.sum(jnp.where(sc >= trial, 1, 0), axis=1, keepdims=True)

    thr = _kth_largest(count_ge, (tq, 1), topk)
    n_gt = jnp.sum(jnp.where(sc > thr, 1, 0), axis=1, keepdims=True)
    need = (topk - n_gt).astype(F32)
    tie, _ = _tie_select(sc == thr, need, _tri128())
    sel = vis & ((sc > thr) | (tie > 0))

    k_all = jnp.concatenate([ck_ref[0].astype(BF16), kn_ref[...], jnp.zeros((pad, HW), BF16)], axis=0)
    v_all = jnp.concatenate([cv_ref[0].astype(BF16), vn_ref[...], jnp.zeros((pad, HW), BF16)], axis=0)
    qh = _split_heads_q(q_ref[...])
    o_ref[...] = _direct_attend([_dot_nt(qh[h], k_all) for h in range(A_HEADS)], sel, v_all)


def _dsa_sample(aq, iq, sm, akb, avb, cache_k, cache_v, cache_ik, *, n_prompt, dbatch, dseq, topk):
    past = cache_k.shape[1]
    base = n_prompt // dseq
    qrow = lambda w: pl.BlockSpec((dseq, w), lambda b: (base + b, 0))
    crow = lambda w: pl.BlockSpec((1, past, w), lambda b: (b, 0, 0))
    kern = functools.partial(_dsa_sample_kernel, tq=dseq, past=past, topk=topk)
    return pl.pallas_call(
        kern,
        grid=(dbatch,),
        in_specs=[qrow(HW), qrow(LANES), qrow(LANES), qrow(HW), qrow(HW),
                  crow(HW), crow(HW), crow(IDX_DIM)],
        out_specs=pl.BlockSpec((dseq, HW), lambda b: (b, 0)),
        out_shape=jax.ShapeDtypeStruct((dbatch * dseq, HW), BF16),
        compiler_params=_cparams(("parallel",)),
        name="dsa_sample",
    )(aq, iq, sm, akb, avb, cache_k, cache_v, cache_ik)


def _split3(x):
    hi = x.astype(BF16)
    r1 = x - hi.astype(F32)
    mid = r1.astype(BF16)
    lo = (r1 - mid.astype(F32)).astype(BF16)
    return hi, mid, lo


def _fox_prompt_kernel(qt_ref, k_ref, vt_ref, smk_ref, o_ref, ck3_ref, acc_ref, *, tq, tk, seq):
    j = pl.program_id(1)
    tri = _tri(tk)

    @pl.when(j == 0)
    def _():
        lane = lax.broadcasted_iota(I32, (tk, LANES), 1)
        is_f = (lane >= SM_BF) & (lane < SM_BF + B_HEADS)
        carry = jnp.zeros((1, LANES), F32)
        for r in range(seq // tk):
            lf = jnp.where(is_f, smk_ref[r * tk:(r + 1) * tk, :], 0.0)
            cs = carry
            for piece in _split3(lf):
                cs = cs + _dot(tri, piece)
            carry = cs[tk - 1:tk, :]
            for i, piece in enumerate(_split3(-cs)):
                ck3_ref[r * tk:(r + 1) * tk, i * LANES:(i + 1) * LANES] = piece

    q0 = j * tq
    row = lax.broadcasted_iota(I32, (tk, tq), 0)
    col = lax.broadcasted_iota(I32, (tk, tq), 1)
    qpos = q0 + col
    qh = _split_heads_qt(qt_ref[...])
    srow = lax.broadcasted_iota(I32, (3 * LANES, tq), 0) & (LANES - 1)
    pick = [jnp.where(srow == SM_BF + h, 1.0, 0.0).astype(BF16) for h in range(B_HEADS)]

    def sel_fn(off):
        return (off + row) <= qpos

    def logits_fn(off):
        k_blk = k_ref[pl.ds(off, tk), :]
        ck3 = ck3_ref[pl.ds(off, tk), :]
        return [_dot(k_blk, qh[h]) + _dot(ck3, pick[h]) for h in range(B_HEADS)]

    o_ref[...] = _flash_t(j + 1, tk, tq, logits_fn, sel_fn, vt_ref, acc_ref)


def _fox_prompt(bqt, bkb, bvt, sm, *, batch, seq, tq):
    nq = seq // tq
    tk = tq
    krow = lambda w: pl.BlockSpec((seq, w), lambda b, j: (b, 0))
    kern = functools.partial(_fox_prompt_kernel, tq=tq, tk=tk, seq=seq)
    return pl.pallas_call(
        kern,
        grid=(batch, nq),
        in_specs=[pl.BlockSpec((HW, tq), lambda b, j: (0, b * nq + j)), krow(HW),
                  pl.BlockSpec((HW, seq), lambda b, j: (0, b)), krow(LANES)],
        out_specs=pl.BlockSpec((tq, HW), lambda b, j: (b * nq + j, 0)),
        out_shape=jax.ShapeDtypeStruct((batch * seq, HW), BF16),
        scratch_shapes=[pltpu.VMEM((seq, 3 * LANES), BF16), pltpu.VMEM((HW, tq), F32)],
        compiler_params=_cparams(("parallel", "arbitrary")),
        name="fox_prompt",
    )(bqt, bkb, bvt, sm)


def _fox_sample_kernel(q_ref, kn_ref, vn_ref, ck_ref, cv_ref, clf_ref, nlf_ref, o_ref, *, tq, past):
    pad = LANES - tq
    cum_c = _lane_cumsum(clf_ref[0])
    cum_n = _lane_cumsum(nlf_ref[0])
    bias_c = cum_c[:, past - 1:past] - cum_c
    bias_n = -cum_n
    bias = jnp.concatenate([bias_c, bias_n], axis=1)
    lk = past + LANES
    row = lax.broadcasted_iota(I32, (tq, lk), 0)
    col = lax.broadcasted_iota(I32, (tq, lk), 1)
    sel = col <= past + row
    k_all = jnp.concatenate([ck_ref[0].astype(BF16), kn_ref[...], jnp.zeros((pad, HW), BF16)], axis=0)
    v_all = jnp.concatenate([cv_ref[0].astype(BF16), vn_ref[...], jnp.zeros((pad, HW), BF16)], axis=0)
    qh = _split_heads_q(q_ref[...])
    logits = [_dot_nt(qh[h], k_all) + bias[h:h + 1, :] for h in range(B_HEADS)]
    o_ref[...] = _direct_attend(logits, sel, v_all)


def _fox_sample(bq, bkb, bvb, cache_k, cache_v, cache_lft, new_lft, *, n_prompt, dbatch, dseq):
    past = cache_k.shape[1]
    base = n_prompt // dseq
    qrow = lambda w: pl.BlockSpec((dseq, w), lambda b: (base + b, 0))
    crow = lambda w: pl.BlockSpec((1, past, w), lambda b: (b, 0, 0))
    kern = functools.partial(_fox_sample_kernel, tq=dseq, past=past)
    return pl.pallas_call(
        kern,
        grid=(dbatch,),
        in_specs=[qrow(HW), qrow(HW), qrow(HW), crow(HW), crow(HW),
                  pl.BlockSpec((1, 8, past), lambda b: (b, 0, 0)),
                  pl.BlockSpec((1, 8, LANES), lambda b: (b, 0, 0))],
        out_specs=pl.BlockSpec((dseq, HW), lambda b: (b, 0)),
        out_shape=jax.ShapeDtypeStruct((dbatch * dseq, HW), BF16),
        compiler_params=_cparams(("parallel",)),
        name="fox_sample",
    )(bq, bkb, bvb, cache_k, cache_v, cache_lft, new_lft)


def _mla_logits(q, kfull_blk):
    return [_dot_nt(q[:, h * LANES:(h + 1) * LANES], kfull_blk[:, h * LANES:(h + 1) * LANES])
            for h in range(C_HEADS)]


def _mla_prompt_kernel(qt_ref, ckv_ref, ckvt_ref, smk_ref, w1_ref, w2_ref, w1vt_ref, o_ref,
                       kf_ref, vt_ref, acc_ref, *, tq, tk, seq):
    j = pl.program_id(1)

    @pl.when(j == 0)
    def _():
        rc = 512 if seq % 512 == 0 else tk
        for r in range(seq // rc):
            kf = (_dot(ckv_ref[r * rc:(r + 1) * rc, :].astype(BF16), w1_ref[:, 0:512])
                  + _dot(smk_ref[r * rc:(r + 1) * rc, :].astype(BF16), w2_ref[:, 0:512]))
            kf_ref[r * rc:(r + 1) * rc, :] = kf.astype(BF16)
            vt_ref[:, r * rc:(r + 1) * rc] = _dot(w1vt_ref[...], ckvt_ref[:, r * rc:(r + 1) * rc]).astype(BF16)

    q0 = j * tq
    row = lax.broadcasted_iota(I32, (tk, tq), 0)
    col = lax.broadcasted_iota(I32, (tk, tq), 1)
    qchunk = (q0 + col) >> 6
    qt = qt_ref[...]

    def sel_fn(off):
        return ((off + row) >> 6) <= qchunk

    def logits_fn(off):
        kf = kf_ref[pl.ds(off, tk), :]
        return [_dot(kf[:, h * LANES:(h + 1) * LANES], qt[h * LANES:(h + 1) * LANES, :])
                for h in range(C_HEADS)]

    o_ref[...] = _flash_t(j + 1, tk, tq, logits_fn, sel_fn, vt_ref, acc_ref)


def _mla_prompt(qct, ckv, ckvt, sm, w1, w2, w1vt, *, batch, seq, tq):
    nq = seq // tq
    tk = tq
    krow = lambda w: pl.BlockSpec((seq, w), lambda b, j: (b, 0))
    full = lambda a: pl.BlockSpec(a.shape, lambda b, j: (0,) * a.ndim)
    kern = functools.partial(_mla_prompt_kernel, tq=tq, tk=tk, seq=seq)
    return pl.pallas_call(
        kern,
        grid=(batch, nq),
        in_specs=[pl.BlockSpec((512, tq), lambda b, j: (0, b * nq + j)), krow(LANES),
                  pl.BlockSpec((LANES, seq), lambda b, j: (0, b)), krow(LANES),
                  full(w1), full(w2), full(w1vt)],
        out_specs=pl.BlockSpec((tq, HW), lambda b, j: (b * nq + j, 0)),
        out_shape=jax.ShapeDtypeStruct((batch * seq, HW), BF16),
        scratch_shapes=[pltpu.VMEM((seq, 512), BF16), pltpu.VMEM((HW, seq), BF16),
                        pltpu.VMEM((HW, tq), F32)],
        compiler_params=_cparams(("parallel", "arbitrary")),
        name="mla_prompt",
    )(qct, ckv, ckvt, sm, w1, w2, w1vt)


def _mla_sample_kernel(q_ref, ckvn_ref, smn_ref, cckv_ref, ckr_ref, w1_ref, w2_ref, w2c_ref, o_ref,
                       *, tq, past):
    pad = LANES - tq
    kv_c = _dot(cckv_ref[0].astype(BF16), w1_ref[...]) + _dot(ckr_ref[0].astype(BF16), w2c_ref[...])
    kv_n = _dot(ckvn_ref[...].astype(BF16), w1_ref[...]) + _dot(smn_ref[...].astype(BF16), w2_ref[...])
    kv = jnp.concatenate([kv_c, kv_n, jnp.zeros((pad, 768), F32)], axis=0).astype(BF16)
    lk = past + LANES
    col = lax.broadcasted_iota(I32, (tq, lk), 1)
    sel = col < past + tq
    o_ref[...] = _direct_attend(_mla_logits(q_ref[...], kv[:, :512]), sel, kv[:, 512:])


def _mla_sample(qc, ckv, sm, cache_ckv, cache_kr, w1, w2, w2c, *, n_prompt, dbatch, dseq):
    past = cache_ckv.shape[1]
    base = n_prompt // dseq
    qrow = lambda w: pl.BlockSpec((dseq, w), lambda b: (base + b, 0))
    crow = lambda w: pl.BlockSpec((1, past, w), lambda b: (b, 0, 0))
    full = lambda a: pl.BlockSpec(a.shape, lambda b: (0,) * a.ndim)
    kern = functools.partial(_mla_sample_kernel, tq=dseq, past=past)
    return pl.pallas_call(
        kern,
        grid=(dbatch,),
        in_specs=[qrow(512), qrow(LANES), qrow(LANES), crow(C_KV_RANK), crow(C_ROPE),
                  full(w1), full(w2), full(w2c)],
        out_specs=pl.BlockSpec((dseq, HW), lambda b: (b, 0)),
        out_shape=jax.ShapeDtypeStruct((dbatch * dseq, HW), BF16),
        compiler_params=_cparams(("parallel",)),
        name="mla_sample",
    )(qc, ckv, sm, cache_ckv, cache_kr, w1, w2, w2c)


def _merge_kernel(x_ref, g1_ref, oa_ref, ob_ref, oc_ref, wg_ref, wua_ref, wub_ref, wuc_ref, wout_ref,
                  xo_ref):
    x = x_ref[...]
    d = x.shape[1]
    h = _rmsnorm(x, g1_ref[...]).astype(BF16)
    gates = jax.nn.sigmoid(_dot(h, wg_ref[...]))
    merged = (gates[:, 0:d] * _dot(oa_ref[...], wua_ref[...])
              + gates[:, d:2 * d] * _dot(ob_ref[...], wub_ref[...])
              + gates[:, 2 * d:3 * d] * _dot(oc_ref[...], wuc_ref[...]))
    xo_ref[...] = x + _dot(merged.astype(BF16), wout_ref[...])


def _merge(x, g1, oa, ob, oc, wg, wua, wub, wuc, wout, *, tm):
    n, d = x.shape
    row = lambda w: pl.BlockSpec((tm, w), lambda i: (i, 0))
    full = lambda a: pl.BlockSpec(a.shape, lambda i: (0,) * a.ndim)
    return pl.pallas_call(
        _merge_kernel,
        grid=(n // tm,),
        in_specs=[row(d), full(g1), row(HW), row(HW), row(HW),
                  full(wg), full(wua), full(wub), full(wuc), full(wout)],
        out_specs=row(d),
        out_shape=jax.ShapeDtypeStruct((n, d), F32),
        compiler_params=_cparams(("parallel",)),
        name="merge",
    )(x, g1, oa, ob, oc, wg, wua, wub, wuc, wout)


WORD = I32


def _row_words(x):
    u = pltpu.bitcast(x.astype(BF16).astype(F32), I32)
    return u | lax.shift_right_logical(u, 16)


def _bcast_rows_bf16(words_row, rows):
    return pltpu.bitcast(jnp.broadcast_to(words_row, (rows // 2, words_row.shape[1])), BF16)


def _top_values(s, k, with_rank=False):
    vals = []
    rank = jnp.full(s.shape, float(k), F32)
    for r in range(k):
        m = jnp.max(s, axis=0, keepdims=True)
        vals.append(m)
        hit = s == m
        if with_rank:
            rank = jnp.where(hit, float(r), rank)
        s = jnp.where(hit, -jnp.inf, s)
    return (vals, rank) if with_rank else vals


def _peer_kernel(x_ref, g2_ref, wqt_ref, k1_ref, k2_ref, u_ref, vt_ref, o_ref,
                 ht_ref, qt_ref, acc_ref, rank_ref, e2_ref, cnt_ref, c_ref, v2_ref, cand_ref,
                 *, nk, na, grp):
    k = pl.program_id(1)
    tm = x_ref.shape[0]
    half = PEER_QDIM // 2

    @pl.when(k == 0)
    def _():
        h = _rmsnorm(x_ref[...], g2_ref[...])
        ht = h.T.astype(BF16)
        ht_ref[...] = ht
        qt_ref[...] = _dot(wqt_ref[...], ht).astype(BF16)
        acc_ref[...] = jnp.zeros(acc_ref.shape, F32)

        def select_body(hh, carry):
            qrow = pl.multiple_of(hh * PEER_QDIM, PEER_QDIM)
            krow = pl.multiple_of(hh * nk, nk)
            q1 = qt_ref[pl.ds(qrow, half), :]
            q2 = qt_ref[pl.ds(qrow + half, half), :]
            s1 = _dot(k1_ref[pl.ds(krow, nk), :], q1)
            s2 = _dot(k2_ref[pl.ds(krow, nk), :], q2)
            v1 = _top_values(s1, PEER_TOPK)
            v2, rank2 = _top_values(s2, PEER_TOPK, with_rank=True)
            for jj in range(PEER_TOPK):
                v2_ref[jj:jj + 1, :] = v2[jj]
            cand_ref[...] = jnp.full(cand_ref.shape, -jnp.inf, F32)
            off = 0
            for i in range(PEER_TOPK):
                nj = PEER_TOPK // (i + 1)
                cand_ref[off:off + nj, :] = v1[i] + v2_ref[0:nj, :]
                off += nj
            cand = cand_ref[...]
            tau = _top_values(cand, PEER_TOPK)[-1]
            top = v1[0] + v2[0]
            z = jnp.sum(jnp.where(cand >= tau, jnp.exp(cand - top), 0.0), axis=0, keepdims=True)
            cnt = jnp.zeros((nk, tm), F32)
            for jj in range(PEER_TOPK):
                cnt = jnp.where((s1 + v2[jj]) >= tau, float(jj + 1), cnt)
            rank_ref[hh] = rank2.astype(BF16)
            e2_ref[hh] = jnp.exp(s2 - v2[0]).astype(BF16)
            cnt_ref[hh] = _row_words(cnt)
            c_ref[hh] = _row_words(jnp.exp(s1 - v1[0]) / z)
            return carry

        lax.fori_loop(0, PEER_HEADS, select_body, 0)

    a0 = pl.multiple_of(k * na, na)
    cnt_rows = [cnt_ref[hh, pl.ds(a0, na), :] for hh in range(PEER_HEADS)]
    c_rows = [c_ref[hh, pl.ds(a0, na), :] for hh in range(PEER_HEADS)]
    zero = jnp.zeros((nk, tm), BF16)
    total = None
    for p in range(na // grp):
        rows = slice(p * grp * nk, (p + 1) * grp * nk)
        act = jax.nn.gelu(_dot(u_ref[rows, :], ht_ref[...])).astype(BF16)
        gated = []
        for ai in range(p * grp, (p + 1) * grp):
            w = None
            for hh in range(PEER_HEADS):
                cb = _bcast_rows_bf16(cnt_rows[hh][ai:ai + 1, :], nk)
                cc = _bcast_rows_bf16(c_rows[hh][ai:ai + 1, :], nk)
                t = jnp.where(rank_ref[hh] < cb, e2_ref[hh], zero) * cc
                w = t if w is None else w + t
            r0 = (ai - p * grp) * nk
            gated.append(w * act[r0:r0 + nk, :])
        part = _dot(vt_ref[:, rows], jnp.concatenate(gated, axis=0))
        total = part if total is None else total + part
    acc_ref[...] += total

    @pl.when(k == pl.num_programs(1) - 1)
    def _():
        o_ref[...] = x_ref[...] + acc_ref[...].T


def _peer(x, g2, wqt, k1, k2, u, vt, *, tm, na):
    n, d = x.shape
    nk = k1.shape[0] // PEER_HEADS
    te = na * nk
    nchunks = nk // na
    ncand = sum(PEER_TOPK // (i + 1) for i in range(PEER_TOPK))
    ncand = -(-ncand // 8) * 8
    full = lambda a: pl.BlockSpec(a.shape, lambda i, k: (0,) * a.ndim)
    kern = functools.partial(_peer_kernel, nk=nk, na=na, grp=na)
    return pl.pallas_call(
        kern,
        grid=(n // tm, nchunks),
        in_specs=[pl.BlockSpec((tm, d), lambda i, k: (i, 0)), full(g2), full(wqt), full(k1), full(k2),
                  pl.BlockSpec((te, d), lambda i, k: (k, 0)),
                  pl.BlockSpec((d, te), lambda i, k: (0, k))],
        out_specs=pl.BlockSpec((tm, d), lambda i, k: (i, 0)),
        out_shape=jax.ShapeDtypeStruct((n, d), F32),
        scratch_shapes=[pltpu.VMEM((d, tm), BF16), pltpu.VMEM((PEER_HEADS * PEER_QDIM, tm), BF16),
                        pltpu.VMEM((d, tm), F32),
                        pltpu.VMEM((PEER_HEADS, nk, tm), BF16), pltpu.VMEM((PEER_HEADS, nk, tm), BF16),
                        pltpu.VMEM((PEER_HEADS, nk, tm), WORD), pltpu.VMEM((PEER_HEADS, nk, tm), WORD),
                        pltpu.VMEM((PEER_TOPK, tm), F32), pltpu.VMEM((ncand, tm), F32)],
        compiler_params=_cparams(("parallel", "arbitrary")),
        name="peer",
    )(x, g2, wqt, k1, k2, u, vt)


def _final_norm_kernel(x_ref, g_ref, o_ref):
    o_ref[...] = _rmsnorm(x_ref[...], g_ref[...])


def _final_norm(x, g, *, tm):
    n, d = x.shape
    return pl.pallas_call(
        _final_norm_kernel,
        grid=(n // tm,),
        in_specs=[pl.BlockSpec((tm, d), lambda i: (i, 0)), pl.BlockSpec((1, d), lambda i: (0, 0))],
        out_specs=pl.BlockSpec((tm, d), lambda i: (i, 0)),
        out_shape=jax.ShapeDtypeStruct((n, d), F32),
        compiler_params=_cparams(("parallel",)),
        name="final_norm",
    )(x, g)


def _rope_tables(pos):
    rows = pos.shape[0]

    def cs(half):
        inv = jnp.power(ROPE_THETA, -jnp.arange(half, dtype=F32) / half)
        ang = pos.astype(F32)[:, None] * inv[None, :]
        c, s = jnp.cos(ang), jnp.sin(ang)
        return jnp.concatenate([c, c], axis=1), jnp.concatenate([-s, s], axis=1)

    c64, s64 = cs(HEAD_DIM // 2)
    c32, s32 = cs(IDX_DIM // 2)
    one32 = jnp.ones((rows, 32), F32)
    zero32 = jnp.zeros((rows, 32), F32)
    c_sm = jnp.concatenate([c32, c32, one32, one32], axis=1)
    s_sm = jnp.concatenate([s32, s32, zero32, zero32], axis=1)
    c_qc = jnp.concatenate([one32, one32, c32, one32], axis=1)
    s_qc = jnp.concatenate([zero32, zero32, s32, zero32], axis=1)
    ctab = jnp.concatenate([jnp.tile(c64, (1, 4)), jnp.tile(c32, (1, 4)), c_sm, jnp.tile(c_qc, (1, 4))], axis=1)
    stab = jnp.concatenate([jnp.tile(s64, (1, 4)), jnp.tile(s32, (1, 4)), s_sm, jnp.tile(s_qc, (1, 4))], axis=1)
    return ctab, stab


def _split_w_in(w_in):
    sizes = (HW, HW, HW, IDX_HEADS * IDX_DIM, IDX_DIM, IDX_HEADS, HW, HW, HW, B_HEADS,
             C_Q_RANK, C_KV_RANK, C_ROPE)
    offs = np.cumsum((0,) + sizes)
    parts = [w_in[..., int(offs[i]):int(offs[i + 1])] for i in range(len(sizes))]
    gates = w_in[..., int(offs[-1]):]
    return parts, gates


def _pack_weights(w_in, w_uq, w_ukv, b_forget):
    depth, d, _ = w_in.shape
    (a_q, a_k, a_v, i_q, i_k, i_w, b_q, b_k, b_v, b_f, c_q, c_kv, c_kr), gates = _split_w_in(w_in)
    z = lambda w: jnp.zeros((depth, d, w), F32)
    small = jnp.concatenate([i_k, c_kr, b_f, z(4), i_w, z(LANES - SM_IW - IDX_HEADS)], axis=-1)
    wmain = jnp.concatenate([a_q, a_k, a_v, b_q, b_k, b_v, c_q, c_kv, i_q, small], axis=-1).astype(BF16)
    wg = gates.astype(BF16)

    uq = w_uq.reshape(depth, C_Q_RANK, C_HEADS, C_NOPE + C_ROPE)
    uq = jnp.concatenate([uq, jnp.zeros((depth, C_Q_RANK, C_HEADS, 32), F32)], axis=-1)
    wuq = uq.reshape(depth, C_Q_RANK, C_HEADS * LANES).astype(BF16)

    ukv = w_ukv.reshape(depth, C_KV_RANK, C_HEADS, C_NOPE + C_VDIM)
    kn = jnp.concatenate([ukv[..., :C_NOPE], jnp.zeros((depth, C_KV_RANK, C_HEADS, 64), F32)], axis=-1)
    w1 = jnp.concatenate([kn.reshape(depth, C_KV_RANK, 512),
                          ukv[..., C_NOPE:].reshape(depth, C_KV_RANK, HW)], axis=-1).astype(BF16)
    w1vt = jnp.swapaxes(w1[:, :, 512:], 1, 2)
    eye = np.zeros((C_ROPE, 768), np.float32)
    for h in range(C_HEADS):
        for r in range(C_ROPE):
            eye[r, h * LANES + C_NOPE + r] = 1.0
    w2c = jnp.asarray(eye).astype(BF16)
    w2_np = np.zeros((LANES, 768), np.float32)
    w2_np[SM_KR:SM_KR + C_ROPE] = eye
    w2 = jnp.asarray(w2_np).astype(BF16)

    aux = jnp.zeros((depth, 8, LANES), F32)
    aux = aux.at[:, 0, SM_BF:SM_BF + B_HEADS].set(b_forget)
    scale = np.ones((LANES,), np.float32)
    scale[SM_IW:SM_IW + IDX_HEADS] = (IDX_HEADS * IDX_DIM) ** -0.5
    aux = aux.at[:, 1, :].set(jnp.asarray(scale))
    return wmain, wg, wuq, w1, w1vt, w2, w2c, aux


def _tile_rows(n):
    for t in (512, 256, 128):
        if n % t == 0:
            return t
    raise ValueError(f"token count {n} is not a multiple of 128")


def kernel(x_prompt, x_sample, cache_k_a, cache_v_a, cache_kidx_a, cache_k_b, cache_v_b, cache_logf_b,
           cache_ckv_c, cache_krope_c, norm1_g, w_in, c_q_norm_g, c_kv_norm_g, w_uq, w_ukv, b_forget,
           w_up_a, w_up_b, w_up_c, w_out, norm2_g, peer_w_q, peer_keys, peer_u, peer_v, final_norm_g):
    batch, seq, d = x_prompt.shape
    dbatch, dseq, _ = x_sample.shape
    depth = w_in.shape[0]
    past = cache_k_a.shape[2]
    n_p = batch * seq
    n_s = dbatch * dseq
    n = n_p + n_s
    nk = peer_keys.shape[3]
    assert dseq == CHUNK and past % LANES == 0 and seq % 256 == 0

    tm = min(_tile_rows(seq), _tile_rows(n_s))
    tq = 512 if seq % 512 == 0 else 256
    topk_p = min(TOPK_MAX, seq // 4)
    topk_s = min(TOPK_MAX, (past + dseq) // 4)

    pos = jnp.concatenate([jnp.arange(seq, dtype=jnp.int32),
                           past + (jnp.arange(tm, dtype=jnp.int32) % dseq)])
    ctab, stab = _rope_tables(pos)

    wmain, wg, wuq, w1, w1vt, w2, w2c, aux = _pack_weights(w_in, w_uq, w_ukv, b_forget)
    wua, wub, wuc, wout = (w.astype(BF16) for w in (w_up_a, w_up_b, w_up_c, w_out))
    wqt = jnp.swapaxes(peer_w_q, 1, 2).astype(BF16)
    half = PEER_QDIM // 2
    k1 = peer_keys[:, :, 0].reshape(depth, PEER_HEADS * nk, half).astype(BF16)
    k2 = peer_keys[:, :, 1].reshape(depth, PEER_HEADS * nk, half).astype(BF16)
    pu = peer_u.astype(BF16)
    pvt = jnp.swapaxes(peer_v, 1, 2).astype(BF16)
    clf_t = jnp.swapaxes(cache_logf_b, 2, 3)
    clf_t = jnp.concatenate([clf_t, jnp.zeros_like(clf_t)], axis=2)

    g_row = lambda g: g.reshape(1, -1)
    x = jnp.concatenate([x_prompt.reshape(n_p, d), x_sample.reshape(n_s, d)], axis=0)
    rows_p = [[] for _ in range(8)]
    rows_s = [[] for _ in range(8)]
    na = 8 if nk % 8 == 0 else nk

    for l in range(depth):
        g1 = g_row(norm1_g[l])
        (aq, akf, akb, avf, avb, bq, bkf, bkb, bvf, bvb, qc, ckv, iq, sm, lft,
         aqt, avt, bqt, bvt, qct, iqt, ckvt, iwt) = _inproj(
            x, g1, wmain[l], wuq[l], g_row(c_q_norm_g[l]), g_row(c_kv_norm_g[l]), aux[l], ctab, stab,
            tm=tm, n_prompt=n_p, seq=seq)

        new_lft = lft[:, n_p:].reshape(8, dbatch, dseq).transpose(1, 0, 2)
        new_lft = jnp.concatenate([new_lft, jnp.zeros((dbatch, 8, LANES - dseq), F32)], axis=2)

        oa_p = _dsa_prompt(aqt, iqt, iwt, akb, avt, sm, batch=batch, seq=seq, tq=tq, topk=topk_p)
        oa_s = _dsa_sample(aq, iq, sm, akb, avb,
                           cache_k_a[l].reshape(dbatch, past, HW), cache_v_a[l].reshape(dbatch, past, HW),
                           cache_kidx_a[l], n_prompt=n_p, dbatch=dbatch, dseq=dseq, topk=topk_s)
        ob_p = _fox_prompt(bqt, bkb, bvt, sm, batch=batch, seq=seq, tq=tq)
        ob_s = _fox_sample(bq, bkb, bvb,
                           cache_k_b[l].reshape(dbatch, past, HW), cache_v_b[l].reshape(dbatch, past, HW),
                           clf_t[l], new_lft, n_prompt=n_p, dbatch=dbatch, dseq=dseq)
        oc_p = _mla_prompt(qct, ckv, ckvt, sm, w1[l], w2, w1vt[l], batch=batch, seq=seq, tq=tq)
        oc_s = _mla_sample(qc, ckv, sm, cache_ckv_c[l], cache_krope_c[l], w1[l], w2, w2c,
                           n_prompt=n_p, dbatch=dbatch, dseq=dseq)
        oa = jnp.concatenate([oa_p, oa_s], axis=0)
        ob = jnp.concatenate([ob_p, ob_s], axis=0)
        oc = jnp.concatenate([oc_p, oc_s], axis=0)

        x = _merge(x, g1, oa, ob, oc, wg[l], wua[l], wub[l], wuc[l], wout[l], tm=tm)
        x = _peer(x, g_row(norm2_g[l]), wqt[l], k1[l], k2[l], pu[l], pvt[l], tm=tm, na=na)

        new = (akf, avf, sm[:, SM_IK:SM_IK + IDX_DIM], bkf, bvf, sm[:, SM_BF:SM_BF + B_HEADS],
               ckv, sm[:, SM_KR:SM_KR + C_ROPE])
        for i, a in enumerate(new):
            rows_p[i].append(a[:n_p])
            rows_s[i].append(a[n_p:])

    y = _final_norm(x, g_row(final_norm_g), tm=tm)
    y_p = y[:n_p].reshape(batch, seq, d)
    y_s = y[n_p:].reshape(dbatch, dseq, d)

    tails = [(A_HEADS, HEAD_DIM), (A_HEADS, HEAD_DIM), (IDX_DIM,), (B_HEADS, HEAD_DIM), (B_HEADS, HEAD_DIM),
             (B_HEADS,), (C_KV_RANK,), (C_ROPE,)]
    outs_p = [jnp.stack(r).reshape((depth, batch, seq) + t) for r, t in zip(rows_p, tails)]
    outs_s = [jnp.stack(r).reshape((depth, dbatch, dseq) + t) for r, t in zip(rows_s, tails)]
    return (y_p, y_s, *outs_p, *outs_s)
```

```python
import functools

import jax
import jax.numpy as jnp
import numpy as np
from jax import lax
from jax.experimental import pallas as pl
from jax.experimental.pallas import tpu as pltpu

F32 = jnp.float32
BF16 = jnp.bfloat16
I32 = jnp.int32

CHUNK = 64
HEAD_DIM = 64
ROPE_THETA = 10000.0
NORM_EPS = 1e-6
A_HEADS = 4
IDX_HEADS = 4
IDX_DIM = 32
TOPK_MAX = 256
B_HEADS = 4
C_HEADS = 4
C_NOPE = 64
C_ROPE = 32
C_VDIM = 64
C_Q_RANK = 256
C_KV_RANK = 128
N_BRANCHES = 3
PEER_HEADS = 8
PEER_QDIM = 128
PEER_TOPK = 16

HW = A_HEADS * HEAD_DIM
LANES = 128
VMEM_LIMIT = 56 * 1024 * 1024

SM_IK = 0
SM_KR = 32
SM_BF = 64
SM_IW = 72

MC_AQ, MC_AK, MC_AV = 0, 256, 512
MC_BQ, MC_BK, MC_BV = 768, 1024, 1280
MC_CQ = 1536
MC_CKV = 1792
MC_IQ = 1920
MC_SM = 2048
MAIN_COLS = 2176

RT_64 = 0
RT_IQ = 256
RT_SM = 384
RT_QC = 512
RT_COLS = 1024

NEG = -1e30
KEY_NEG_INF = -2139095041
INT_MIN = -2147483648


def _cparams(sem):
    return pltpu.CompilerParams(dimension_semantics=sem, vmem_limit_bytes=VMEM_LIMIT)


def _rmsnorm(x, g):
    ms = jnp.mean(x * x, axis=-1, keepdims=True)
    return x * lax.rsqrt(ms + NORM_EPS) * g


def _dot(a, b):
    return jnp.dot(a, b, preferred_element_type=F32)


def _dot_nt(a, b):
    return lax.dot_general(a, b, (((1,), (1,)), ((), ())), preferred_element_type=F32)


def _rope(v, c, s, half):
    w = v.shape[-1]
    lane = lax.broadcasted_iota(I32, v.shape, 1)
    first = (lane & (2 * half - 1)) < half
    vr = jnp.where(first, pltpu.roll(v, w - half, 1), pltpu.roll(v, half, 1))
    return v * c + vr * s


def _inproj_kernel(x_ref, g1_ref, wmain_ref, wuq_ref, gq_ref, gkv_ref, aux_ref, ct_ref, st_ref,
                   aq_ref, akb_ref, avb_ref, bq_ref, bkb_ref, bvb_ref,
                   qc_ref, ckv_ref, iq_ref, sm_ref, lft_ref,
                   aqt_ref, avt_ref, bqt_ref, bvt_ref, qct_ref, iqt_ref, ckvt_ref, iwt_ref,
                   *row_refs, npt):
    x = x_ref[...]
    h = _rmsnorm(x, g1_ref[...]).astype(BF16)
    y = _dot(h, wmain_ref[...])
    ct = ct_ref[...]
    st = st_ref[...]

    c64 = ct[:, RT_64:RT_64 + HW]
    s64 = st[:, RT_64:RT_64 + HW]
    aq = _rope(y[:, MC_AQ:MC_AQ + HW], c64, s64, HEAD_DIM // 2)
    aq = aq * (HEAD_DIM ** -0.5)
    aq_ref[...] = aq.astype(BF16)
    aqt_ref[...] = aq.T.astype(BF16)
    ak = _rope(y[:, MC_AK:MC_AK + HW], c64, s64, HEAD_DIM // 2)
    akb_ref[...] = ak.astype(BF16)
    av = y[:, MC_AV:MC_AV + HW]
    avb_ref[...] = av.astype(BF16)
    avt_ref[...] = av.T.astype(BF16)

    bq = y[:, MC_BQ:MC_BQ + HW] * (HEAD_DIM ** -0.5)
    bq_ref[...] = bq.astype(BF16)
    bqt_ref[...] = bq.T.astype(BF16)
    bk = y[:, MC_BK:MC_BK + HW]
    bkb_ref[...] = bk.astype(BF16)
    bv = y[:, MC_BV:MC_BV + HW]
    bvb_ref[...] = bv.astype(BF16)
    bvt_ref[...] = bv.T.astype(BF16)

    cq = _rmsnorm(y[:, MC_CQ:MC_CQ + C_Q_RANK], gq_ref[...]).astype(BF16)
    qc = _dot(cq, wuq_ref[...])
    qc = _rope(qc, ct[:, RT_QC:RT_QC + 512], st[:, RT_QC:RT_QC + 512], C_ROPE // 2)
    qc = qc * ((C_NOPE + C_ROPE) ** -0.5)
    qc_ref[...] = qc.astype(BF16)
    qct_ref[...] = qc.T.astype(BF16)

    ckv = _rmsnorm(y[:, MC_CKV:MC_CKV + C_KV_RANK], gkv_ref[...])
    ckv_ref[...] = ckv
    ckvt_ref[...] = ckv.T.astype(BF16)

    iq = _rope(y[:, MC_IQ:MC_IQ + LANES], ct[:, RT_IQ:RT_IQ + LANES],
               st[:, RT_IQ:RT_IQ + LANES], IDX_DIM // 2)
    iq_ref[...] = iq
    iqt_ref[...] = iq.T.astype(BF16)

    sm = _rope(y[:, MC_SM:MC_SM + LANES], ct[:, RT_SM:RT_SM + LANES],
               st[:, RT_SM:RT_SM + LANES], IDX_DIM // 2)
    lane = lax.broadcasted_iota(I32, sm.shape, 1)
    is_f = (lane >= SM_BF) & (lane < SM_BF + B_HEADS)
    z = sm + aux_ref[0:1, :]
    logsig = jnp.minimum(z, 0.0) - jnp.log1p(jnp.exp(-jnp.abs(z)))
    sm = jnp.where(is_f, logsig, sm * aux_ref[1:2, :])
    sm_ref[...] = sm
    smt = sm.T
    lft_ref[...] = smt[SM_BF:SM_BF + 8, :]
    iwt_ref[...] = smt[SM_IW:SM_IW + 8, :]

    new_rows = (ak, av, sm[:, SM_IK:SM_IK + IDX_DIM], bk, bv, sm[:, SM_BF:SM_BF + B_HEADS],
                ckv, sm[:, SM_KR:SM_KR + C_ROPE])
    nr = len(new_rows)
    i = pl.program_id(0)

    @pl.when(i < npt)
    def _():
        for r, v in zip(row_refs[:nr], new_rows):
            r[...] = v

    @pl.when(i >= npt)
    def _():
        for r, v in zip(row_refs[nr:], new_rows):
            r[...] = v


ROW_WIDTHS = (HW, HW, IDX_DIM, HW, HW, B_HEADS, C_KV_RANK, C_ROPE)


def _inproj(x, g1, wmain, wuq, gq, gkv, aux, ctab, stab, *, tm, n_prompt, seq):
    n, d = x.shape
    nt = n // tm
    npt = n_prompt // tm
    tps = seq // tm

    def tmap(i):
        return (jnp.where(i < npt, i % tps, tps), 0)

    row = lambda w: pl.BlockSpec((tm, w), lambda i: (i, 0))
    prow = lambda w: pl.BlockSpec((tm, w), lambda i: (jnp.minimum(i, npt - 1), 0))
    srow = lambda w: pl.BlockSpec((tm, w), lambda i: (jnp.maximum(i - npt, 0), 0))
    full = lambda a: pl.BlockSpec(a.shape, lambda i: (0,) * a.ndim)
    out_shapes = [
        jax.ShapeDtypeStruct((n, HW), BF16),
        jax.ShapeDtypeStruct((n, HW), BF16),
        jax.ShapeDtypeStruct((n, HW), BF16),
        jax.ShapeDtypeStruct((n, HW), BF16),
        jax.ShapeDtypeStruct((n, HW), BF16),
        jax.ShapeDtypeStruct((n, HW), BF16),
        jax.ShapeDtypeStruct((n, 512), BF16),
        jax.ShapeDtypeStruct((n, LANES), F32),
        jax.ShapeDtypeStruct((n, LANES), F32),
        jax.ShapeDtypeStruct((n, LANES), F32),
        jax.ShapeDtypeStruct((8, n), F32),
        jax.ShapeDtypeStruct((HW, n), BF16),
        jax.ShapeDtypeStruct((HW, n), BF16),
        jax.ShapeDtypeStruct((HW, n), BF16),
        jax.ShapeDtypeStruct((HW, n), BF16),
        jax.ShapeDtypeStruct((512, n), BF16),
        jax.ShapeDtypeStruct((LANES, n), BF16),
        jax.ShapeDtypeStruct((LANES, n), BF16),
        jax.ShapeDtypeStruct((8, n), F32),
    ]
    out_shapes += [jax.ShapeDtypeStruct((n_prompt, w), F32) for w in ROW_WIDTHS]
    out_shapes += [jax.ShapeDtypeStruct((n - n_prompt, w), F32) for w in ROW_WIDTHS]
    col = lambda w: pl.BlockSpec((w, tm), lambda i: (0, i))
    out_specs = [row(HW)] * 6 + [row(512), row(LANES), row(LANES), row(LANES), col(8),
                                 col(HW), col(HW), col(HW), col(HW), col(512), col(LANES), col(LANES),
                                 col(8)]
    out_specs += [prow(w) for w in ROW_WIDTHS] + [srow(w) for w in ROW_WIDTHS]
    return pl.pallas_call(
        functools.partial(_inproj_kernel, npt=npt),
        grid=(nt,),
        in_specs=[row(d), full(g1), full(wmain), full(wuq), full(gq), full(gkv), full(aux),
                  pl.BlockSpec((tm, RT_COLS), tmap), pl.BlockSpec((tm, RT_COLS), tmap)],
        out_specs=out_specs,
        out_shape=out_shapes,
        compiler_params=_cparams(("arbitrary",)),
        name="inproj",
    )(x, g1, wmain, wuq, gq, gkv, aux, ctab, stab)


def _head_lane_mask(shape, h, width=HEAD_DIM):
    lane = lax.broadcasted_iota(I32, shape, 1)
    return (lane >= h * width) & (lane < (h + 1) * width)


def _split_heads_q(q):
    return [jnp.where(_head_lane_mask(q.shape, h), q, jnp.zeros_like(q)) for h in range(4)]


def _flash_init(m_ref, l_ref, acc_ref):
    m_ref[...] = jnp.full(m_ref.shape, NEG, F32)
    l_ref[...] = jnp.zeros(l_ref.shape, F32)
    acc_ref[...] = jnp.zeros(acc_ref.shape, F32)


def _flash_update(h, logits, sel, v_blk, m_ref, l_ref, acc_ref):
    lm = jnp.where(sel, logits, NEG)
    m_old = m_ref[h]
    m_new = jnp.maximum(m_old, jnp.max(lm, axis=1, keepdims=True))
    p = jnp.where(sel, jnp.exp(logits - m_new), 0.0)
    alpha = jnp.exp(m_old - m_new)
    l_ref[h] = alpha * l_ref[h] + jnp.sum(p, axis=1, keepdims=True)
    acc_ref[h] = alpha * acc_ref[h] + _dot(p.astype(BF16), v_blk)
    m_ref[h] = m_new


def _flash_finish(l_ref, acc_ref):
    out = None
    for h in range(4):
        o = acc_ref[h] / l_ref[h]
        o = jnp.where(_head_lane_mask(o.shape, h), o, 0.0)
        out = o if out is None else out + o
    return out.astype(BF16)


def _direct_attend(logits_list, sel, v_blk):
    out = None
    for h, logits in enumerate(logits_list):
        lm = jnp.where(sel, logits, NEG)
        m = jnp.max(lm, axis=1, keepdims=True)
        p = jnp.where(sel, jnp.exp(logits - m), 0.0)
        l = jnp.sum(p, axis=1, keepdims=True)
        o = _dot(p.astype(BF16), v_blk) / l
        o = jnp.where(_head_lane_mask(o.shape, h), o, 0.0)
        out = o if out is None else out + o
    return out.astype(BF16)


def _lane_cumsum(x):
    n = x.shape[1]
    lane = lax.broadcasted_iota(I32, x.shape, 1)
    k = 1
    while k < n:
        x = x + jnp.where(lane >= k, pltpu.roll(x, k, 1), 0.0)
        k *= 2
    return x


def _key_to_float(key):
    bits = key ^ ((key >> 31) & 0x7FFFFFFF)
    return jnp.where(key < KEY_NEG_INF, -jnp.inf, pltpu.bitcast(bits, F32))


def _kth_largest(count_ge, shape, k):
    zero = jnp.zeros(shape, I32)
    c0 = count_ge(_key_to_float(zero))
    lo = jnp.where(c0 >= k, zero, jnp.full(shape, INT_MIN, I32))

    def bit_body(i, lo):
        trial = lo + jnp.left_shift(jnp.int32(1), 30 - i)
        c = count_ge(_key_to_float(trial))
        return jnp.where(c >= k, trial, lo)

    lo = lax.fori_loop(0, 31, bit_body, lo)
    return _key_to_float(lo)


def _tri128():
    r = lax.broadcasted_iota(I32, (LANES, LANES), 0)
    c = lax.broadcasted_iota(I32, (LANES, LANES), 1)
    return jnp.where(r <= c, 1.0, 0.0).astype(BF16)


def _tri(n):
    r = lax.broadcasted_iota(I32, (n, n), 0)
    c = lax.broadcasted_iota(I32, (n, n), 1)
    return jnp.where(c <= r, 1.0, 0.0).astype(BF16)


def _split_heads_qt(qt):
    row = lax.broadcasted_iota(I32, qt.shape, 0)
    return [jnp.where((row >= h * HEAD_DIM) & (row < (h + 1) * HEAD_DIM), qt, jnp.zeros_like(qt))
            for h in range(4)]


def _flash_t(nkb, tk, tq, logits_fn, sel_fn, vt_ref, acc_ref):
    acc_ref[...] = jnp.zeros(acc_ref.shape, F32)

    def body(kb, carry):
        ms, ls = carry
        off = pl.multiple_of(kb * tk, tk)
        sel = sel_fn(off)
        lgs = logits_fn(off)
        new_m, new_l = [], []
        for h in range(4):
            lm = jnp.where(sel, lgs[h], NEG)
            m_new = jnp.maximum(ms[h], jnp.max(lm, axis=0, keepdims=True))
            p = jnp.exp(lm - m_new)
            alpha = jnp.exp(ms[h] - m_new)
            new_l.append(alpha * ls[h] + jnp.sum(p, axis=0, keepdims=True))
            rows = slice(h * HEAD_DIM, (h + 1) * HEAD_DIM)
            acc_ref[rows, :] = alpha * acc_ref[rows, :] + _dot(vt_ref[rows, pl.ds(off, tk)], p.astype(BF16))
            new_m.append(m_new)
        return tuple(new_m), tuple(new_l)

    init = (tuple(jnp.full((1, tq), NEG / 2, F32) for _ in range(4)),
            tuple(jnp.zeros((1, tq), F32) for _ in range(4)))
    _, ls = lax.fori_loop(0, nkb, body, init)
    out = jnp.concatenate([acc_ref[h * HEAD_DIM:(h + 1) * HEAD_DIM, :] / ls[h] for h in range(4)], axis=0)
    return out.T.astype(BF16)


def _tie_select(eq, rem, tri):
    parts = []
    for c in range(eq.shape[1] // LANES):
        e = eq[:, c * LANES:(c + 1) * LANES]
        pre = _dot(jnp.where(e, 1.0, 0.0).astype(BF16), tri)
        parts.append(jnp.where(e & (pre <= rem), 1, 0))
        rem = rem - pre[:, LANES - 1:LANES]
    return jnp.concatenate(parts, axis=1), rem


def _dsa_query_prep(iq, smq):
    lane = lax.broadcasted_iota(I32, iq.shape, 1)
    iqh = []
    for h in range(IDX_HEADS):
        r = iq if h == 0 else pltpu.roll(iq, LANES - IDX_DIM * h, 1)
        iqh.append(jnp.where(lane < IDX_DIM, r, 0.0).astype(BF16))
    wcol = [smq[:, SM_IW + h:SM_IW + h + 1] for h in range(IDX_HEADS)]
    return iqh, wcol


def _dsa_scores(iqh, wcol, ik_blk, contract32=False):
    sc = None
    for h in range(IDX_HEADS):
        a = iqh[h][:, :IDX_DIM] if contract32 else iqh[h]
        z = _dot_nt(a, ik_blk)
        t = jnp.maximum(z, 0.0) * wcol[h]
        sc = t if sc is None else sc + t
    return sc


def _dsa_prompt_kernel(qt_ref, iqt_ref, iwt_ref, k_ref, vt_ref, smk_ref, o_ref,
                       sc_ref, acc_ref, rem_ref, *, tq, tk, topk):
    j = pl.program_id(1)
    nkb = j + 1
    q0 = j * tq
    iqt = iqt_ref[...]
    pad = jnp.zeros((LANES - IDX_DIM, tq), BF16)
    iqh = [jnp.concatenate([iqt[IDX_DIM * h:IDX_DIM * (h + 1), :], pad], axis=0) for h in range(IDX_HEADS)]
    iw = iwt_ref[...]
    row = lax.broadcasted_iota(I32, (tk, tq), 0)
    col = lax.broadcasted_iota(I32, (tk, tq), 1)
    qchunk = (q0 + col) >> 6

    def visible(off):
        return ((off + row) >> 6) <= qchunk

    def score_body(kb, carry):
        off = pl.multiple_of(kb * tk, tk)
        ik = smk_ref[pl.ds(off, tk), :].astype(BF16)
        sc = None
        for h in range(IDX_HEADS):
            t = jnp.maximum(_dot(ik, iqh[h]), 0.0) * iw[h:h + 1, :]
            sc = t if sc is None else sc + t
        sc_ref[pl.ds(off, tk), :] = jnp.where(visible(off), sc, -jnp.inf)
        return carry

    lax.fori_loop(0, nkb, score_body, 0)

    def count_cmp(trial, strict):
        def body(kb, c):
            off = pl.multiple_of(kb * tk, tk)
            s = sc_ref[pl.ds(off, tk), :]
            hit = (s > trial) if strict else (s >= trial)
            return c + jnp.sum(jnp.where(hit, 1, 0), axis=0, keepdims=True)
        return lax.fori_loop(0, nkb, body, jnp.zeros((1, tq), I32))

    thr = _kth_largest(lambda t: count_cmp(t, False), (1, tq), topk)
    rem_ref[0:1, :] = (topk - count_cmp(thr, True)).astype(F32)
    qh = _split_heads_qt(qt_ref[...])
    tri = _tri(tk)

    def sel_fn(off):
        s = sc_ref[pl.ds(off, tk), :]
        eq = s == thr
        pre = _dot(tri, jnp.where(eq, 1.0, 0.0).astype(BF16))
        rem = rem_ref[0:1, :]
        keep = eq & (pre <= rem)
        rem_ref[0:1, :] = rem - pre[tk - 1:tk, :]
        return visible(off) & ((s > thr) | keep)

    def logits_fn(off):
        k_blk = k_ref[pl.ds(off, tk), :]
        return [_dot(k_blk, qh[h]) for h in range(A_HEADS)]

    o_ref[...] = _flash_t(nkb, tk, tq, logits_fn, sel_fn, vt_ref, acc_ref)


def _dsa_prompt(aqt, iqt, iwt, akb, avt, sm, *, batch, seq, tq, topk, n_out):
    nq = seq // tq
    tk = tq
    qcol = lambda w: pl.BlockSpec((w, tq), lambda b, j: (0, b * nq + j))
    krow = lambda w: pl.BlockSpec((seq, w), lambda b, j: (b, 0))
    kern = functools.partial(_dsa_prompt_kernel, tq=tq, tk=tk, topk=topk)
    return pl.pallas_call(
        kern,
        grid=(batch, nq),
        in_specs=[qcol(HW), qcol(LANES), qcol(8), krow(HW),
                  pl.BlockSpec((HW, seq), lambda b, j: (0, b)), krow(LANES)],
        out_specs=pl.BlockSpec((tq, HW), lambda b, j: (b * nq + j, 0)),
        out_shape=jax.ShapeDtypeStruct((n_out, HW), BF16),
        scratch_shapes=[pltpu.VMEM((seq, tq), F32), pltpu.VMEM((HW, tq), F32), pltpu.VMEM((8, tq), F32)],
        compiler_params=_cparams(("parallel", "arbitrary")),
        name="dsa_prompt",
    )(aqt, iqt, iwt, akb, avt, sm)


def _dsa_sample_kernel(q_ref, iq_ref, smq_ref, kn_ref, vn_ref, ck_ref, cv_ref, cik_ref, prev_ref, o_ref,
                       *, tq, past, topk):
    pad = LANES - tq
    iqh, wcol = _dsa_query_prep(iq_ref[...], smq_ref[...])
    sc_c = _dsa_scores(iqh, wcol, cik_ref[0].astype(BF16), contract32=True)
    ik_new = jnp.concatenate([smq_ref[...].astype(BF16), jnp.zeros((pad, LANES), BF16)], axis=0)
    sc_n = _dsa_scores(iqh, wcol, ik_new)
    lane = lax.broadcasted_iota(I32, (tq, LANES), 1)
    sc_n = jnp.where(lane < tq, sc_n, -jnp.inf)
    sc = jnp.concatenate([sc_c, sc_n], axis=1)
    lk = past + LANES
    col = lax.broadcasted_iota(I32, (tq, lk), 1)
    vis = col < past + tq

    def count_ge(trial):
        return jnp.sum(jnp.where(sc >= trial, 1, 0), axis=1, keepdims=True)

    thr = _kth_largest(count_ge, (tq, 1), topk)
    n_gt = jnp.sum(jnp.where(sc > thr, 1, 0), axis=1, keepdims=True)
    need = (topk - n_gt).astype(F32)
    tie, _ = _tie_select(sc == thr, need, _tri128())
    sel = vis & ((sc > thr) | (tie > 0))

    k_all = jnp.concatenate([ck_ref[0].astype(BF16), kn_ref[...], jnp.zeros((pad, HW), BF16)], axis=0)
    v_all = jnp.concatenate([cv_ref[0].astype(BF16), vn_ref[...], jnp.zeros((pad, HW), BF16)], axis=0)
    qh = _split_heads_q(q_ref[...])
    o_ref[...] = _direct_attend([_dot_nt(qh[h], k_all) for h in range(A_HEADS)], sel, v_all)


def _dsa_sample(aq, iq, sm, akb, avb, cache_k, cache_v, cache_ik, o_prev, *, n_prompt, dbatch, dseq, topk):
    past = cache_k.shape[1]
    base = n_prompt // dseq
    qrow = lambda w: pl.BlockSpec((dseq, w), lambda b: (base + b, 0))
    crow = lambda w: pl.BlockSpec((1, past, w), lambda b: (b, 0, 0))
    kern = functools.partial(_dsa_sample_kernel, tq=dseq, past=past, topk=topk)
    return pl.pallas_call(
        kern,
        grid=(dbatch,),
        in_specs=[qrow(HW), qrow(LANES), qrow(LANES), qrow(HW), qrow(HW),
                  crow(HW), crow(HW), crow(IDX_DIM), pl.BlockSpec(memory_space=pl.ANY)],
        out_specs=qrow(HW),
        out_shape=jax.ShapeDtypeStruct(o_prev.shape, BF16),
        input_output_aliases={8: 0},
        compiler_params=_cparams(("parallel",)),
        name="dsa_sample",
    )(aq, iq, sm, akb, avb, cache_k, cache_v, cache_ik, o_prev)


def _split3(x):
    hi = x.astype(BF16)
    r1 = x - hi.astype(F32)
    mid = r1.astype(BF16)
    lo = (r1 - mid.astype(F32)).astype(BF16)
    return hi, mid, lo


def _fox_prompt_kernel(qt_ref, k_ref, vt_ref, smk_ref, o_ref, ck3_ref, acc_ref, *, tq, tk, seq):
    j = pl.program_id(1)
    tri = _tri(tk)

    @pl.when(j == 0)
    def _():
        lane = lax.broadcasted_iota(I32, (tk, LANES), 1)
        is_f = (lane >= SM_BF) & (lane < SM_BF + B_HEADS)
        carry = jnp.zeros((1, LANES), F32)
        for r in range(seq // tk):
            lf = jnp.where(is_f, smk_ref[r * tk:(r + 1) * tk, :], 0.0)
            cs = carry
            for piece in _split3(lf):
                cs = cs + _dot(tri, piece)
            carry = cs[tk - 1:tk, :]
            for i, piece in enumerate(_split3(-cs)):
                ck3_ref[r * tk:(r + 1) * tk, i * LANES:(i + 1) * LANES] = piece

    q0 = j * tq
    row = lax.broadcasted_iota(I32, (tk, tq), 0)
    col = lax.broadcasted_iota(I32, (tk, tq), 1)
    qpos = q0 + col
    qh = _split_heads_qt(qt_ref[...])
    srow = lax.broadcasted_iota(I32, (3 * LANES, tq), 0) & (LANES - 1)
    pick = [jnp.where(srow == SM_BF + h, 1.0, 0.0).astype(BF16) for h in range(B_HEADS)]

    def sel_fn(off):
        return (off + row) <= qpos

    def logits_fn(off):
        k_blk = k_ref[pl.ds(off, tk), :]
        ck3 = ck3_ref[pl.ds(off, tk), :]
        return [_dot(k_blk, qh[h]) + _dot(ck3, pick[h]) for h in range(B_HEADS)]

    o_ref[...] = _flash_t(j + 1, tk, tq, logits_fn, sel_fn, vt_ref, acc_ref)


def _fox_prompt(bqt, bkb, bvt, sm, *, batch, seq, tq, n_out):
    nq = seq // tq
    tk = tq
    krow = lambda w: pl.BlockSpec((seq, w), lambda b, j: (b, 0))
    kern = functools.partial(_fox_prompt_kernel, tq=tq, tk=tk, seq=seq)
    return pl.pallas_call(
        kern,
        grid=(batch, nq),
        in_specs=[pl.BlockSpec((HW, tq), lambda b, j: (0, b * nq + j)), krow(HW),
                  pl.BlockSpec((HW, seq), lambda b, j: (0, b)), krow(LANES)],
        out_specs=pl.BlockSpec((tq, HW), lambda b, j: (b * nq + j, 0)),
        out_shape=jax.ShapeDtypeStruct((n_out, HW), BF16),
        scratch_shapes=[pltpu.VMEM((seq, 3 * LANES), BF16), pltpu.VMEM((HW, tq), F32)],
        compiler_params=_cparams(("parallel", "arbitrary")),
        name="fox_prompt",
    )(bqt, bkb, bvt, sm)


def _fox_sample_kernel(q_ref, kn_ref, vn_ref, ck_ref, cv_ref, clf_ref, nlf_ref, prev_ref, o_ref, *, tq, past):
    pad = LANES - tq
    cum_c = _lane_cumsum(clf_ref[0])
    cum_n = _lane_cumsum(nlf_ref[0])
    bias_c = cum_c[:, past - 1:past] - cum_c
    bias_n = -cum_n
    bias = jnp.concatenate([bias_c, bias_n], axis=1)
    lk = past + LANES
    row = lax.broadcasted_iota(I32, (tq, lk), 0)
    col = lax.broadcasted_iota(I32, (tq, lk), 1)
    sel = col <= past + row
    k_all = jnp.concatenate([ck_ref[0].astype(BF16), kn_ref[...], jnp.zeros((pad, HW), BF16)], axis=0)
    v_all = jnp.concatenate([cv_ref[0].astype(BF16), vn_ref[...], jnp.zeros((pad, HW), BF16)], axis=0)
    qh = _split_heads_q(q_ref[...])
    logits = [_dot_nt(qh[h], k_all) + bias[h:h + 1, :] for h in range(B_HEADS)]
    o_ref[...] = _direct_attend(logits, sel, v_all)


def _fox_sample(bq, bkb, bvb, cache_k, cache_v, cache_lft, new_lft, o_prev, *, n_prompt, dbatch, dseq):
    past = cache_k.shape[1]
    base = n_prompt // dseq
    qrow = lambda w: pl.BlockSpec((dseq, w), lambda b: (base + b, 0))
    crow = lambda w: pl.BlockSpec((1, past, w), lambda b: (b, 0, 0))
    kern = functools.partial(_fox_sample_kernel, tq=dseq, past=past)
    return pl.pallas_call(
        kern,
        grid=(dbatch,),
        in_specs=[qrow(HW), qrow(HW), qrow(HW), crow(HW), crow(HW),
                  pl.BlockSpec((1, 8, past), lambda b: (b, 0, 0)),
                  pl.BlockSpec((1, 8, LANES), lambda b: (b, 0, 0)),
                  pl.BlockSpec(memory_space=pl.ANY)],
        out_specs=qrow(HW),
        out_shape=jax.ShapeDtypeStruct(o_prev.shape, BF16),
        input_output_aliases={7: 0},
        compiler_params=_cparams(("parallel",)),
        name="fox_sample",
    )(bq, bkb, bvb, cache_k, cache_v, cache_lft, new_lft, o_prev)


def _mla_logits(q, kfull_blk):
    return [_dot_nt(q[:, h * LANES:(h + 1) * LANES], kfull_blk[:, h * LANES:(h + 1) * LANES])
            for h in range(C_HEADS)]


def _mla_prompt_kernel(qt_ref, ckv_ref, ckvt_ref, smk_ref, w1_ref, w2_ref, w1vt_ref, o_ref,
                       kf_ref, vt_ref, acc_ref, *, tq, tk, seq):
    j = pl.program_id(1)

    @pl.when(j == 0)
    def _():
        rc = 512 if seq % 512 == 0 else tk
        for r in range(seq // rc):
            kf = (_dot(ckv_ref[r * rc:(r + 1) * rc, :].astype(BF16), w1_ref[:, 0:512])
                  + _dot(smk_ref[r * rc:(r + 1) * rc, :].astype(BF16), w2_ref[:, 0:512]))
            kf_ref[r * rc:(r + 1) * rc, :] = kf.astype(BF16)
            vt_ref[:, r * rc:(r + 1) * rc] = _dot(w1vt_ref[...], ckvt_ref[:, r * rc:(r + 1) * rc]).astype(BF16)

    q0 = j * tq
    row = lax.broadcasted_iota(I32, (tk, tq), 0)
    col = lax.broadcasted_iota(I32, (tk, tq), 1)
    qchunk = (q0 + col) >> 6
    qt = qt_ref[...]

    def sel_fn(off):
        return ((off + row) >> 6) <= qchunk

    def logits_fn(off):
        kf = kf_ref[pl.ds(off, tk), :]
        return [_dot(kf[:, h * LANES:(h + 1) * LANES], qt[h * LANES:(h + 1) * LANES, :])
                for h in range(C_HEADS)]

    o_ref[...] = _flash_t(j + 1, tk, tq, logits_fn, sel_fn, vt_ref, acc_ref)


def _mla_prompt(qct, ckv, ckvt, sm, w1, w2, w1vt, *, batch, seq, tq, n_out):
    nq = seq // tq
    tk = tq
    krow = lambda w: pl.BlockSpec((seq, w), lambda b, j: (b, 0))
    full = lambda a: pl.BlockSpec(a.shape, lambda b, j: (0,) * a.ndim)
    kern = functools.partial(_mla_prompt_kernel, tq=tq, tk=tk, seq=seq)
    return pl.pallas_call(
        kern,
        grid=(batch, nq),
        in_specs=[pl.BlockSpec((512, tq), lambda b, j: (0, b * nq + j)), krow(LANES),
                  pl.BlockSpec((LANES, seq), lambda b, j: (0, b)), krow(LANES),
                  full(w1), full(w2), full(w1vt)],
        out_specs=pl.BlockSpec((tq, HW), lambda b, j: (b * nq + j, 0)),
        out_shape=jax.ShapeDtypeStruct((n_out, HW), BF16),
        scratch_shapes=[pltpu.VMEM((seq, 512), BF16), pltpu.VMEM((HW, seq), BF16),
                        pltpu.VMEM((HW, tq), F32)],
        compiler_params=_cparams(("parallel", "arbitrary")),
        name="mla_prompt",
    )(qct, ckv, ckvt, sm, w1, w2, w1vt)


def _mla_sample_kernel(q_ref, ckvn_ref, smn_ref, cckv_ref, ckr_ref, w1_ref, w2_ref, w2c_ref, prev_ref, o_ref,
                       *, tq, past):
    pad = LANES - tq
    kv_c = _dot(cckv_ref[0].astype(BF16), w1_ref[...]) + _dot(ckr_ref[0].astype(BF16), w2c_ref[...])
    kv_n = _dot(ckvn_ref[...].astype(BF16), w1_ref[...]) + _dot(smn_ref[...].astype(BF16), w2_ref[...])
    kv = jnp.concatenate([kv_c, kv_n, jnp.zeros((pad, 768), F32)], axis=0).astype(BF16)
    lk = past + LANES
    col = lax.broadcasted_iota(I32, (tq, lk), 1)
    sel = col < past + tq
    o_ref[...] = _direct_attend(_mla_logits(q_ref[...], kv[:, :512]), sel, kv[:, 512:])


def _mla_sample(qc, ckv, sm, cache_ckv, cache_kr, w1, w2, w2c, o_prev, *, n_prompt, dbatch, dseq):
    past = cache_ckv.shape[1]
    base = n_prompt // dseq
    qrow = lambda w: pl.BlockSpec((dseq, w), lambda b: (base + b, 0))
    crow = lambda w: pl.BlockSpec((1, past, w), lambda b: (b, 0, 0))
    full = lambda a: pl.BlockSpec(a.shape, lambda b: (0,) * a.ndim)
    kern = functools.partial(_mla_sample_kernel, tq=dseq, past=past)
    return pl.pallas_call(
        kern,
        grid=(dbatch,),
        in_specs=[qrow(512), qrow(LANES), qrow(LANES), crow(C_KV_RANK), crow(C_ROPE),
                  full(w1), full(w2), full(w2c), pl.BlockSpec(memory_space=pl.ANY)],
        out_specs=qrow(HW),
        out_shape=jax.ShapeDtypeStruct(o_prev.shape, BF16),
        input_output_aliases={8: 0},
        compiler_params=_cparams(("parallel",)),
        name="mla_sample",
    )(qc, ckv, sm, cache_ckv, cache_kr, w1, w2, w2c, o_prev)


def _merge_kernel(x_ref, g1_ref, oa_ref, ob_ref, oc_ref, wg_ref, wua_ref, wub_ref, wuc_ref, wout_ref,
                  xo_ref):
    x = x_ref[...]
    d = x.shape[1]
    h = _rmsnorm(x, g1_ref[...]).astype(BF16)
    gates = jax.nn.sigmoid(_dot(h, wg_ref[...]))
    merged = (gates[:, 0:d] * _dot(oa_ref[...], wua_ref[...])
              + gates[:, d:2 * d] * _dot(ob_ref[...], wub_ref[...])
              + gates[:, 2 * d:3 * d] * _dot(oc_ref[...], wuc_ref[...]))
    xo_ref[...] = x + _dot(merged.astype(BF16), wout_ref[...])


def _merge(x, g1, oa, ob, oc, wg, wua, wub, wuc, wout, *, tm):
    n, d = x.shape
    row = lambda w: pl.BlockSpec((tm, w), lambda i: (i, 0))
    full = lambda a: pl.BlockSpec(a.shape, lambda i: (0,) * a.ndim)
    return pl.pallas_call(
        _merge_kernel,
        grid=(n // tm,),
        in_specs=[row(d), full(g1), row(HW), row(HW), row(HW),
                  full(wg), full(wua), full(wub), full(wuc), full(wout)],
        out_specs=row(d),
        out_shape=jax.ShapeDtypeStruct((n, d), F32),
        compiler_params=_cparams(("parallel",)),
        name="merge",
    )(x, g1, oa, ob, oc, wg, wua, wub, wuc, wout)


WORD = I32


def _row_words(x):
    u = pltpu.bitcast(x.astype(BF16).astype(F32), I32)
    return u | lax.shift_right_logical(u, 16)


def _bcast_rows_bf16(words_row, rows):
    return pltpu.bitcast(jnp.broadcast_to(words_row, (rows // 2, words_row.shape[1])), BF16)


def _top_values(s, k, with_rank=False):
    vals = []
    rank = jnp.full(s.shape, float(k), F32)
    for r in range(k):
        m = jnp.max(s, axis=0, keepdims=True)
        vals.append(m)
        hit = s == m
        if with_rank:
            rank = jnp.where(hit, float(r), rank)
        s = jnp.where(hit, -jnp.inf, s)
    return (vals, rank) if with_rank else vals


def _peer_kernel(x_ref, g2_ref, wqt_ref, k1_ref, k2_ref, u_ref, vt_ref, o_ref,
                 ht_ref, qt_ref, acc_ref, rank_ref, e2_ref, cnt_ref, c_ref, v2_ref, cand_ref,
                 *, nk, na, grp):
    k = pl.program_id(1)
    tm = x_ref.shape[0]
    half = PEER_QDIM // 2

    @pl.when(k == 0)
    def _():
        h = _rmsnorm(x_ref[...], g2_ref[...])
        ht = h.T.astype(BF16)
        ht_ref[...] = ht
        qt_ref[...] = _dot(wqt_ref[...], ht).astype(BF16)
        acc_ref[...] = jnp.zeros(acc_ref.shape, F32)

        def select_body(hh, carry):
            qrow = pl.multiple_of(hh * PEER_QDIM, PEER_QDIM)
            krow = pl.multiple_of(hh * nk, nk)
            q1 = qt_ref[pl.ds(qrow, half), :]
            q2 = qt_ref[pl.ds(qrow + half, half), :]
            s1 = _dot(k1_ref[pl.ds(krow, nk), :], q1)
            s2 = _dot(k2_ref[pl.ds(krow, nk), :], q2)
            v1 = _top_values(s1, PEER_TOPK)
            v2, rank2 = _top_values(s2, PEER_TOPK, with_rank=True)
            for jj in range(PEER_TOPK):
                v2_ref[jj:jj + 1, :] = v2[jj]
            cand_ref[...] = jnp.full(cand_ref.shape, -jnp.inf, F32)
            off = 0
            for i in range(PEER_TOPK):
                nj = PEER_TOPK // (i + 1)
                cand_ref[off:off + nj, :] = v1[i] + v2_ref[0:nj, :]
                off += nj
            cand = cand_ref[...]
            tau = _top_values(cand, PEER_TOPK)[-1]
            top = v1[0] + v2[0]
            z = jnp.sum(jnp.where(cand >= tau, jnp.exp(cand - top), 0.0), axis=0, keepdims=True)
            cnt = jnp.zeros((nk, tm), F32)
            for jj in range(PEER_TOPK):
                cnt = jnp.where((s1 + v2[jj]) >= tau, float(jj + 1), cnt)
            rank_ref[hh] = rank2.astype(BF16)
            e2_ref[hh] = jnp.exp(s2 - v2[0]).astype(BF16)
            cnt_ref[hh] = _row_words(cnt)
            c_ref[hh] = _row_words(jnp.exp(s1 - v1[0]) / z)
            return carry

        lax.fori_loop(0, PEER_HEADS, select_body, 0)

    a0 = pl.multiple_of(k * na, na)
    cnt_rows = [cnt_ref[hh, pl.ds(a0, na), :] for hh in range(PEER_HEADS)]
    c_rows = [c_ref[hh, pl.ds(a0, na), :] for hh in range(PEER_HEADS)]
    zero = jnp.zeros((nk, tm), BF16)
    total = None
    for p in range(na // grp):
        rows = slice(p * grp * nk, (p + 1) * grp * nk)
        act = jax.nn.gelu(_dot(u_ref[rows, :], ht_ref[...])).astype(BF16)
        gated = []
        for ai in range(p * grp, (p + 1) * grp):
            w = None
            for hh in range(PEER_HEADS):
                cb = _bcast_rows_bf16(cnt_rows[hh][ai:ai + 1, :], nk)
                cc = _bcast_rows_bf16(c_rows[hh][ai:ai + 1, :], nk)
                t = jnp.where(rank_ref[hh] < cb, e2_ref[hh], zero) * cc
                w = t if w is None else w + t
            r0 = (ai - p * grp) * nk
            gated.append(w * act[r0:r0 + nk, :])
        part = _dot(vt_ref[:, rows], jnp.concatenate(gated, axis=0))
        total = part if total is None else total + part
    acc_ref[...] += total

    @pl.when(k == pl.num_programs(1) - 1)
    def _():
        o_ref[...] = x_ref[...] + acc_ref[...].T


def _peer(x, g2, wqt, k1, k2, u, vt, *, tm, na):
    n, d = x.shape
    nk = k1.shape[0] // PEER_HEADS
    te = na * nk
    nchunks = nk // na
    ncand = sum(PEER_TOPK // (i + 1) for i in range(PEER_TOPK))
    ncand = -(-ncand // 8) * 8
    full = lambda a: pl.BlockSpec(a.shape, lambda i, k: (0,) * a.ndim)
    kern = functools.partial(_peer_kernel, nk=nk, na=na, grp=na)
    return pl.pallas_call(
        kern,
        grid=(n // tm, nchunks),
        in_specs=[pl.BlockSpec((tm, d), lambda i, k: (i, 0)), full(g2), full(wqt), full(k1), full(k2),
                  pl.BlockSpec((te, d), lambda i, k: (k, 0)),
                  pl.BlockSpec((None, d, te), lambda i, k: (k, 0, 0))],
        out_specs=pl.BlockSpec((tm, d), lambda i, k: (i, 0)),
        out_shape=jax.ShapeDtypeStruct((n, d), F32),
        scratch_shapes=[pltpu.VMEM((d, tm), BF16), pltpu.VMEM((PEER_HEADS * PEER_QDIM, tm), BF16),
                        pltpu.VMEM((d, tm), F32),
                        pltpu.VMEM((PEER_HEADS, nk, tm), BF16), pltpu.VMEM((PEER_HEADS, nk, tm), BF16),
                        pltpu.VMEM((PEER_HEADS, nk, tm), WORD), pltpu.VMEM((PEER_HEADS, nk, tm), WORD),
                        pltpu.VMEM((PEER_TOPK, tm), F32), pltpu.VMEM((ncand, tm), F32)],
        compiler_params=_cparams(("parallel", "arbitrary")),
        name="peer",
    )(x, g2, wqt, k1, k2, u, vt)


def _final_norm_kernel(x_ref, g_ref, op_ref, os_ref, *, npt):
    y = _rmsnorm(x_ref[...], g_ref[...])
    i = pl.program_id(0)

    @pl.when(i < npt)
    def _():
        op_ref[...] = y

    @pl.when(i >= npt)
    def _():
        os_ref[...] = y


def _final_norm(x, g, *, tm, n_prompt):
    n, d = x.shape
    npt = n_prompt // tm
    return pl.pallas_call(
        functools.partial(_final_norm_kernel, npt=npt),
        grid=(n // tm,),
        in_specs=[pl.BlockSpec((tm, d), lambda i: (i, 0)), pl.BlockSpec((1, d), lambda i: (0, 0))],
        out_specs=[pl.BlockSpec((tm, d), lambda i: (jnp.minimum(i, npt - 1), 0)),
                   pl.BlockSpec((tm, d), lambda i: (jnp.maximum(i - npt, 0), 0))],
        out_shape=[jax.ShapeDtypeStruct((n_prompt, d), F32), jax.ShapeDtypeStruct((n - n_prompt, d), F32)],
        compiler_params=_cparams(("arbitrary",)),
        name="final_norm",
    )(x, g)


def _rope_tables(pos):
    rows = pos.shape[0]

    def cs(half):
        inv = jnp.power(ROPE_THETA, -jnp.arange(half, dtype=F32) / half)
        ang = pos.astype(F32)[:, None] * inv[None, :]
        c, s = jnp.cos(ang), jnp.sin(ang)
        return jnp.concatenate([c, c], axis=1), jnp.concatenate([-s, s], axis=1)

    c64, s64 = cs(HEAD_DIM // 2)
    c32, s32 = cs(IDX_DIM // 2)
    one32 = jnp.ones((rows, 32), F32)
    zero32 = jnp.zeros((rows, 32), F32)
    c_sm = jnp.concatenate([c32, c32, one32, one32], axis=1)
    s_sm = jnp.concatenate([s32, s32, zero32, zero32], axis=1)
    c_qc = jnp.concatenate([one32, one32, c32, one32], axis=1)
    s_qc = jnp.concatenate([zero32, zero32, s32, zero32], axis=1)
    ctab = jnp.concatenate([jnp.tile(c64, (1, 4)), jnp.tile(c32, (1, 4)), c_sm, jnp.tile(c_qc, (1, 4))], axis=1)
    stab = jnp.concatenate([jnp.tile(s64, (1, 4)), jnp.tile(s32, (1, 4)), s_sm, jnp.tile(s_qc, (1, 4))], axis=1)
    return ctab, stab


def _split_w_in(w_in):
    sizes = (HW, HW, HW, IDX_HEADS * IDX_DIM, IDX_DIM, IDX_HEADS, HW, HW, HW, B_HEADS,
             C_Q_RANK, C_KV_RANK, C_ROPE)
    offs = np.cumsum((0,) + sizes)
    parts = [w_in[..., int(offs[i]):int(offs[i + 1])] for i in range(len(sizes))]
    gates = w_in[..., int(offs[-1]):]
    return parts, gates


def _pack_weights(w_in, w_uq, w_ukv, b_forget):
    depth, d, _ = w_in.shape
    (a_q, a_k, a_v, i_q, i_k, i_w, b_q, b_k, b_v, b_f, c_q, c_kv, c_kr), gates = _split_w_in(w_in)
    z = lambda w: jnp.zeros((depth, d, w), F32)
    small = jnp.concatenate([i_k, c_kr, b_f, z(4), i_w, z(LANES - SM_IW - IDX_HEADS)], axis=-1)
    wmain = jnp.concatenate([a_q, a_k, a_v, b_q, b_k, b_v, c_q, c_kv, i_q, small], axis=-1).astype(BF16)
    wg = gates.astype(BF16)

    uq = w_uq.reshape(depth, C_Q_RANK, C_HEADS, C_NOPE + C_ROPE)
    uq = jnp.concatenate([uq, jnp.zeros((depth, C_Q_RANK, C_HEADS, 32), F32)], axis=-1)
    wuq = uq.reshape(depth, C_Q_RANK, C_HEADS * LANES).astype(BF16)

    ukv = w_ukv.reshape(depth, C_KV_RANK, C_HEADS, C_NOPE + C_VDIM)
    kn = jnp.concatenate([ukv[..., :C_NOPE], jnp.zeros((depth, C_KV_RANK, C_HEADS, 64), F32)], axis=-1)
    w1 = jnp.concatenate([kn.reshape(depth, C_KV_RANK, 512),
                          ukv[..., C_NOPE:].reshape(depth, C_KV_RANK, HW)], axis=-1).astype(BF16)
    w1vt = jnp.swapaxes(w1[:, :, 512:], 1, 2)
    eye = np.zeros((C_ROPE, 768), np.float32)
    for h in range(C_HEADS):
        for r in range(C_ROPE):
            eye[r, h * LANES + C_NOPE + r] = 1.0
    w2c = jnp.asarray(eye).astype(BF16)
    w2_np = np.zeros((LANES, 768), np.float32)
    w2_np[SM_KR:SM_KR + C_ROPE] = eye
    w2 = jnp.asarray(w2_np).astype(BF16)

    aux = jnp.zeros((depth, 8, LANES), F32)
    aux = aux.at[:, 0, SM_BF:SM_BF + B_HEADS].set(b_forget)
    scale = np.ones((LANES,), np.float32)
    scale[SM_IW:SM_IW + IDX_HEADS] = (IDX_HEADS * IDX_DIM) ** -0.5
    aux = aux.at[:, 1, :].set(jnp.asarray(scale))
    return wmain, wg, wuq, w1, w1vt, w2, w2c, aux


def _tile_rows(n):
    for t in (512, 256, 128):
        if n % t == 0:
            return t
    raise ValueError(f"token count {n} is not a multiple of 128")


def kernel(x_prompt, x_sample, cache_k_a, cache_v_a, cache_kidx_a, cache_k_b, cache_v_b, cache_logf_b,
           cache_ckv_c, cache_krope_c, norm1_g, w_in, c_q_norm_g, c_kv_norm_g, w_uq, w_ukv, b_forget,
           w_up_a, w_up_b, w_up_c, w_out, norm2_g, peer_w_q, peer_keys, peer_u, peer_v, final_norm_g):
    batch, seq, d = x_prompt.shape
    dbatch, dseq, _ = x_sample.shape
    depth = w_in.shape[0]
    past = cache_k_a.shape[2]
    n_p = batch * seq
    n_s = dbatch * dseq
    n = n_p + n_s
    nk = peer_keys.shape[3]
    assert dseq == CHUNK and past % LANES == 0 and seq % 256 == 0

    tm = min(_tile_rows(seq), _tile_rows(n_s))
    tq = 512 if seq % 512 == 0 else 256
    topk_p = min(TOPK_MAX, seq // 4)
    topk_s = min(TOPK_MAX, (past + dseq) // 4)

    pos = jnp.concatenate([jnp.arange(seq, dtype=jnp.int32),
                           past + (jnp.arange(tm, dtype=jnp.int32) % dseq)])
    ctab, stab = _rope_tables(pos)

    wmain, wg, wuq, w1, w1vt, w2, w2c, aux = _pack_weights(w_in, w_uq, w_ukv, b_forget)
    wua, wub, wuc, wout = (w.astype(BF16) for w in (w_up_a, w_up_b, w_up_c, w_out))
    wqt = jnp.swapaxes(peer_w_q, 1, 2).astype(BF16)
    half = PEER_QDIM // 2
    k1 = peer_keys[:, :, 0].reshape(depth, PEER_HEADS * nk, half).astype(BF16)
    k2 = peer_keys[:, :, 1].reshape(depth, PEER_HEADS * nk, half).astype(BF16)
    pu = peer_u.astype(BF16)
    na = 8 if nk % 8 == 0 else nk
    pvt = peer_v.astype(BF16).reshape(depth, nk // na, na * nk, d).transpose(0, 1, 3, 2)
    clf_t = jnp.swapaxes(cache_logf_b, 2, 3)
    clf_t = jnp.concatenate([clf_t, jnp.zeros_like(clf_t)], axis=2)

    g_row = lambda g: g.reshape(1, -1)
    x = jnp.concatenate([x_prompt.reshape(n_p, d), x_sample.reshape(n_s, d)], axis=0)
    nr = len(ROW_WIDTHS)
    rows_p = [[] for _ in range(nr)]
    rows_s = [[] for _ in range(nr)]

    for l in range(depth):
        g1 = g_row(norm1_g[l])
        outs = _inproj(x, g1, wmain[l], wuq[l], g_row(c_q_norm_g[l]), g_row(c_kv_norm_g[l]), aux[l],
                       ctab, stab, tm=tm, n_prompt=n_p, seq=seq)
        (aq, akb, avb, bq, bkb, bvb, qc, ckv, iq, sm, lft,
         aqt, avt, bqt, bvt, qct, iqt, ckvt, iwt) = outs[:19]
        for i in range(nr):
            rows_p[i].append(outs[19 + i])
            rows_s[i].append(outs[19 + nr + i])

        new_lft = lft[:, n_p:].reshape(8, dbatch, dseq).transpose(1, 0, 2)
        new_lft = jnp.concatenate([new_lft, jnp.zeros((dbatch, 8, LANES - dseq), F32)], axis=2)

        oa = _dsa_prompt(aqt, iqt, iwt, akb, avt, sm, batch=batch, seq=seq, tq=tq, topk=topk_p, n_out=n)
        oa = _dsa_sample(aq, iq, sm, akb, avb,
                         cache_k_a[l].reshape(dbatch, past, HW), cache_v_a[l].reshape(dbatch, past, HW),
                         cache_kidx_a[l], oa, n_prompt=n_p, dbatch=dbatch, dseq=dseq, topk=topk_s)
        ob = _fox_prompt(bqt, bkb, bvt, sm, batch=batch, seq=seq, tq=tq, n_out=n)
        ob = _fox_sample(bq, bkb, bvb,
                         cache_k_b[l].reshape(dbatch, past, HW), cache_v_b[l].reshape(dbatch, past, HW),
                         clf_t[l], new_lft, ob, n_prompt=n_p, dbatch=dbatch, dseq=dseq)
        oc = _mla_prompt(qct, ckv, ckvt, sm, w1[l], w2, w1vt[l], batch=batch, seq=seq, tq=tq, n_out=n)
        oc = _mla_sample(qc, ckv, sm, cache_ckv_c[l], cache_krope_c[l], w1[l], w2, w2c, oc,
                         n_prompt=n_p, dbatch=dbatch, dseq=dseq)

        x = _merge(x, g1, oa, ob, oc, wg[l], wua[l], wub[l], wuc[l], wout[l], tm=tm)
        x = _peer(x, g_row(norm2_g[l]), wqt[l], k1[l], k2[l], pu[l], pvt[l], tm=tm, na=na)

    y_p, y_s = _final_norm(x, g_row(final_norm_g), tm=tm, n_prompt=n_p)
    y_p = y_p.reshape(batch, seq, d)
    y_s = y_s.reshape(dbatch, dseq, d)

    tails = [(A_HEADS, HEAD_DIM), (A_HEADS, HEAD_DIM), (IDX_DIM,), (B_HEADS, HEAD_DIM), (B_HEADS, HEAD_DIM),
             (B_HEADS,), (C_KV_RANK,), (C_ROPE,)]
    outs_p = [jnp.stack(r).reshape((depth, batch, seq) + t) for r, t in zip(rows_p, tails)]
    outs_s = [jnp.stack(r).reshape((depth, dbatch, dseq) + t) for r, t in zip(rows_s, tails)]
    return (y_p, y_s, *outs_p, *outs_s)
```

```python
import functools

import jax
import jax.numpy as jnp
import numpy as np
from jax import lax
from jax.experimental import pallas as pl
from jax.experimental.pallas import tpu as pltpu

F32 = jnp.float32
BF16 = jnp.bfloat16
I32 = jnp.int32

CHUNK = 64
HEAD_DIM = 64
ROPE_THETA = 10000.0
NORM_EPS = 1e-6
A_HEADS = 4
IDX_HEADS = 4
IDX_DIM = 32
TOPK_MAX = 256
B_HEADS = 4
C_HEADS = 4
C_NOPE = 64
C_ROPE = 32
C_VDIM = 64
C_Q_RANK = 256
C_KV_RANK = 128
N_BRANCHES = 3
PEER_HEADS = 8
PEER_QDIM = 128
PEER_TOPK = 16

HW = A_HEADS * HEAD_DIM
LANES = 128
VMEM_LIMIT = 56 * 1024 * 1024

SM_IK = 0
SM_KR = 32
SM_BF = 64
SM_IW = 72

MC_AQ, MC_AK, MC_AV = 0, 256, 512
MC_BQ, MC_BK, MC_BV = 768, 1024, 1280
MC_CQ = 1536
MC_CKV = 1792
MC_IQ = 1920
MC_SM = 2048
MAIN_COLS = 2176

RT_64 = 0
RT_IQ = 256
RT_SM = 384
RT_QC = 512
RT_COLS = 1024

NEG = -1e30
KEY_NEG_INF = -2139095041
INT_MIN = -2147483648


def _cparams(sem):
    return pltpu.CompilerParams(dimension_semantics=sem, vmem_limit_bytes=VMEM_LIMIT)


def _rmsnorm(x, g):
    ms = jnp.mean(x * x, axis=-1, keepdims=True)
    return x * lax.rsqrt(ms + NORM_EPS) * g


def _dot(a, b):
    return jnp.dot(a, b, preferred_element_type=F32)


def _dot_nt(a, b):
    return lax.dot_general(a, b, (((1,), (1,)), ((), ())), preferred_element_type=F32)


def _rope(v, c, s, half):
    w = v.shape[-1]
    lane = lax.broadcasted_iota(I32, v.shape, 1)
    first = (lane & (2 * half - 1)) < half
    vr = jnp.where(first, pltpu.roll(v, w - half, 1), pltpu.roll(v, half, 1))
    return v * c + vr * s


def _inproj_kernel(x_ref, g1_ref, wmain_ref, wuq_ref, gq_ref, gkv_ref, aux_ref, ct_ref, st_ref,
                   aq_ref, akb_ref, avb_ref, bq_ref, bkb_ref, bvb_ref,
                   qc_ref, ckv_ref, iq_ref, sm_ref, lft_ref,
                   aqt_ref, avt_ref, bqt_ref, bvt_ref, qct_ref, iqt_ref, ckvt_ref, iwt_ref,
                   *row_refs, npt):
    x = x_ref[...]
    h = _rmsnorm(x, g1_ref[...]).astype(BF16)
    y = _dot(h, wmain_ref[...])
    ct = ct_ref[...]
    st = st_ref[...]

    c64 = ct[:, RT_64:RT_64 + HW]
    s64 = st[:, RT_64:RT_64 + HW]
    aq = _rope(y[:, MC_AQ:MC_AQ + HW], c64, s64, HEAD_DIM // 2)
    aq = aq * (HEAD_DIM ** -0.5)
    aq_ref[...] = aq.astype(BF16)
    aqt_ref[...] = aq.T.astype(BF16)
    ak = _rope(y[:, MC_AK:MC_AK + HW], c64, s64, HEAD_DIM // 2)
    akb_ref[...] = ak.astype(BF16)
    av = y[:, MC_AV:MC_AV + HW]
    avb_ref[...] = av.astype(BF16)
    avt_ref[...] = av.T.astype(BF16)

    bq = y[:, MC_BQ:MC_BQ + HW] * (HEAD_DIM ** -0.5)
    bq_ref[...] = bq.astype(BF16)
    bqt_ref[...] = bq.T.astype(BF16)
    bk = y[:, MC_BK:MC_BK + HW]
    bkb_ref[...] = bk.astype(BF16)
    bv = y[:, MC_BV:MC_BV + HW]
    bvb_ref[...] = bv.astype(BF16)
    bvt_ref[...] = bv.T.astype(BF16)

    cq = _rmsnorm(y[:, MC_CQ:MC_CQ + C_Q_RANK], gq_ref[...]).astype(BF16)
    qc = _dot(cq, wuq_ref[...])
    qc = _rope(qc, ct[:, RT_QC:RT_QC + 512], st[:, RT_QC:RT_QC + 512], C_ROPE // 2)
    qc = qc * ((C_NOPE + C_ROPE) ** -0.5)
    qc_ref[...] = qc.astype(BF16)
    qct_ref[...] = qc.T.astype(BF16)

    ckv = _rmsnorm(y[:, MC_CKV:MC_CKV + C_KV_RANK], gkv_ref[...])
    ckv_ref[...] = ckv
    ckvt_ref[...] = ckv.T.astype(BF16)

    iq = _rope(y[:, MC_IQ:MC_IQ + LANES], ct[:, RT_IQ:RT_IQ + LANES],
               st[:, RT_IQ:RT_IQ + LANES], IDX_DIM // 2)
    iq_ref[...] = iq
    iqt_ref[...] = iq.T.astype(BF16)

    sm = _rope(y[:, MC_SM:MC_SM + LANES], ct[:, RT_SM:RT_SM + LANES],
               st[:, RT_SM:RT_SM + LANES], IDX_DIM // 2)
    lane = lax.broadcasted_iota(I32, sm.shape, 1)
    is_f = (lane >= SM_BF) & (lane < SM_BF + B_HEADS)
    z = sm + aux_ref[0:1, :]
    logsig = jnp.minimum(z, 0.0) - jnp.log1p(jnp.exp(-jnp.abs(z)))
    sm = jnp.where(is_f, logsig, sm * aux_ref[1:2, :])
    sm_ref[...] = sm
    smt = sm.T
    lft_ref[...] = smt[SM_BF:SM_BF + 8, :]
    iwt_ref[...] = smt[SM_IW:SM_IW + 8, :]

    new_rows = (ak, av, sm[:, SM_IK:SM_IK + IDX_DIM], bk, bv, sm[:, SM_BF:SM_BF + B_HEADS],
                ckv, sm[:, SM_KR:SM_KR + C_ROPE])
    nr = len(new_rows)
    i = pl.program_id(0)

    @pl.when(i < npt)
    def _():
        for r, v in zip(row_refs[:nr], new_rows):
            r[...] = v

    @pl.when(i >= npt)
    def _():
        for r, v in zip(row_refs[nr:], new_rows):
            r[...] = v


ROW_WIDTHS = (HW, HW, IDX_DIM, HW, HW, B_HEADS, C_KV_RANK, C_ROPE)


def _inproj(x, g1, wmain, wuq, gq, gkv, aux, ctab, stab, *, tm, n_prompt, seq):
    n, d = x.shape
    nt = n // tm
    npt = n_prompt // tm
    tps = seq // tm

    def tmap(i):
        return (jnp.where(i < npt, i % tps, tps), 0)

    row = lambda w: pl.BlockSpec((tm, w), lambda i: (i, 0))
    prow = lambda w: pl.BlockSpec((tm, w), lambda i: (jnp.minimum(i, npt - 1), 0))
    srow = lambda w: pl.BlockSpec((tm, w), lambda i: (jnp.maximum(i - npt, 0), 0))
    full = lambda a: pl.BlockSpec(a.shape, lambda i: (0,) * a.ndim)
    out_shapes = [
        jax.ShapeDtypeStruct((n, HW), BF16),
        jax.ShapeDtypeStruct((n, HW), BF16),
        jax.ShapeDtypeStruct((n, HW), BF16),
        jax.ShapeDtypeStruct((n, HW), BF16),
        jax.ShapeDtypeStruct((n, HW), BF16),
        jax.ShapeDtypeStruct((n, HW), BF16),
        jax.ShapeDtypeStruct((n, 512), BF16),
        jax.ShapeDtypeStruct((n, LANES), F32),
        jax.ShapeDtypeStruct((n, LANES), F32),
        jax.ShapeDtypeStruct((n, LANES), F32),
        jax.ShapeDtypeStruct((8, n), F32),
        jax.ShapeDtypeStruct((HW, n), BF16),
        jax.ShapeDtypeStruct((HW, n), BF16),
        jax.ShapeDtypeStruct((HW, n), BF16),
        jax.ShapeDtypeStruct((HW, n), BF16),
        jax.ShapeDtypeStruct((512, n), BF16),
        jax.ShapeDtypeStruct((LANES, n), BF16),
        jax.ShapeDtypeStruct((LANES, n), BF16),
        jax.ShapeDtypeStruct((8, n), F32),
    ]
    out_shapes += [jax.ShapeDtypeStruct((n_prompt, w), F32) for w in ROW_WIDTHS]
    out_shapes += [jax.ShapeDtypeStruct((n - n_prompt, w), F32) for w in ROW_WIDTHS]
    col = lambda w: pl.BlockSpec((w, tm), lambda i: (0, i))
    out_specs = [row(HW)] * 6 + [row(512), row(LANES), row(LANES), row(LANES), col(8),
                                 col(HW), col(HW), col(HW), col(HW), col(512), col(LANES), col(LANES),
                                 col(8)]
    out_specs += [prow(w) for w in ROW_WIDTHS] + [srow(w) for w in ROW_WIDTHS]
    return pl.pallas_call(
        functools.partial(_inproj_kernel, npt=npt),
        grid=(nt,),
        in_specs=[row(d), full(g1), full(wmain), full(wuq), full(gq), full(gkv), full(aux),
                  pl.BlockSpec((tm, RT_COLS), tmap), pl.BlockSpec((tm, RT_COLS), tmap)],
        out_specs=out_specs,
        out_shape=out_shapes,
        compiler_params=_cparams(("arbitrary",)),
        name="inproj",
    )(x, g1, wmain, wuq, gq, gkv, aux, ctab, stab)


def _head_lane_mask(shape, h, width=HEAD_DIM):
    lane = lax.broadcasted_iota(I32, shape, 1)
    return (lane >= h * width) & (lane < (h + 1) * width)


def _split_heads_q(q):
    return [jnp.where(_head_lane_mask(q.shape, h), q, jnp.zeros_like(q)) for h in range(4)]


def _flash_init(m_ref, l_ref, acc_ref):
    m_ref[...] = jnp.full(m_ref.shape, NEG, F32)
    l_ref[...] = jnp.zeros(l_ref.shape, F32)
    acc_ref[...] = jnp.zeros(acc_ref.shape, F32)


def _flash_update(h, logits, sel, v_blk, m_ref, l_ref, acc_ref):
    lm = jnp.where(sel, logits, NEG)
    m_old = m_ref[h]
    m_new = jnp.maximum(m_old, jnp.max(lm, axis=1, keepdims=True))
    p = jnp.where(sel, jnp.exp(logits - m_new), 0.0)
    alpha = jnp.exp(m_old - m_new)
    l_ref[h] = alpha * l_ref[h] + jnp.sum(p, axis=1, keepdims=True)
    acc_ref[h] = alpha * acc_ref[h] + _dot(p.astype(BF16), v_blk)
    m_ref[h] = m_new


def _flash_finish(l_ref, acc_ref):
    out = None
    for h in range(4):
        o = acc_ref[h] / l_ref[h]
        o = jnp.where(_head_lane_mask(o.shape, h), o, 0.0)
        out = o if out is None else out + o
    return out.astype(BF16)


def _direct_attend(logits_list, sel, v_blk):
    out = None
    for h, logits in enumerate(logits_list):
        lm = jnp.where(sel, logits, NEG)
        m = jnp.max(lm, axis=1, keepdims=True)
        p = jnp.where(sel, jnp.exp(logits - m), 0.0)
        l = jnp.sum(p, axis=1, keepdims=True)
        o = _dot(p.astype(BF16), v_blk) / l
        o = jnp.where(_head_lane_mask(o.shape, h), o, 0.0)
        out = o if out is None else out + o
    return out.astype(BF16)


def _lane_cumsum(x):
    n = x.shape[1]
    lane = lax.broadcasted_iota(I32, x.shape, 1)
    k = 1
    while k < n:
        x = x + jnp.where(lane >= k, pltpu.roll(x, k, 1), 0.0)
        k *= 2
    return x


def _key_to_float(key):
    bits = key ^ ((key >> 31) & 0x7FFFFFFF)
    return jnp.where(key < KEY_NEG_INF, -jnp.inf, pltpu.bitcast(bits, F32))


def _kth_largest(count_ge, shape, k):
    zero = jnp.zeros(shape, I32)
    c0 = count_ge(_key_to_float(zero))
    lo = jnp.where(c0 >= k, zero, jnp.full(shape, INT_MIN, I32))

    def bit_body(i, lo):
        trial = lo + jnp.left_shift(jnp.int32(1), 30 - i)
        c = count_ge(_key_to_float(trial))
        return jnp.where(c >= k, trial, lo)

    lo = lax.fori_loop(0, 31, bit_body, lo)
    return _key_to_float(lo)


def _kth_largest_multi(count_ge, shape, k, parts):
    zero = jnp.zeros(shape, I32)
    c0 = count_ge(tuple(_key_to_float(zero) for _ in range(parts)))
    lo = tuple(jnp.where(c >= k, zero, jnp.full(shape, INT_MIN, I32)) for c in c0)

    def bit_body(i, lo):
        step = jnp.left_shift(jnp.int32(1), 30 - i)
        trial = tuple(l + step for l in lo)
        c = count_ge(tuple(_key_to_float(t) for t in trial))
        return tuple(jnp.where(ci >= k, ti, li) for ci, ti, li in zip(c, trial, lo))

    lo = lax.fori_loop(0, 31, bit_body, lo)
    return tuple(_key_to_float(l) for l in lo)


def _tri128():
    r = lax.broadcasted_iota(I32, (LANES, LANES), 0)
    c = lax.broadcasted_iota(I32, (LANES, LANES), 1)
    return jnp.where(r <= c, 1.0, 0.0).astype(BF16)


def _tri(n):
    r = lax.broadcasted_iota(I32, (n, n), 0)
    c = lax.broadcasted_iota(I32, (n, n), 1)
    return jnp.where(c <= r, 1.0, 0.0).astype(BF16)


def _split_heads_qt(qt):
    row = lax.broadcasted_iota(I32, qt.shape, 0)
    return [jnp.where((row >= h * HEAD_DIM) & (row < (h + 1) * HEAD_DIM), qt, jnp.zeros_like(qt))
            for h in range(4)]


def _flash_t(nkb, tk, tq, logits_fn, sel_fn, vt_ref, acc_ref):
    acc_ref[...] = jnp.zeros(acc_ref.shape, F32)

    def body(kb, carry):
        ms, ls = carry
        off = pl.multiple_of(kb * tk, tk)
        sel = sel_fn(off)
        lgs = logits_fn(off)
        new_m, new_l = [], []
        for h in range(4):
            lm = jnp.where(sel, lgs[h], NEG)
            m_new = jnp.maximum(ms[h], jnp.max(lm, axis=0, keepdims=True))
            p = jnp.exp(lm - m_new)
            alpha = jnp.exp(ms[h] - m_new)
            new_l.append(alpha * ls[h] + jnp.sum(p, axis=0, keepdims=True))
            rows = slice(h * HEAD_DIM, (h + 1) * HEAD_DIM)
            acc_ref[rows, :] = alpha * acc_ref[rows, :] + _dot(vt_ref[rows, pl.ds(off, tk)], p.astype(BF16))
            new_m.append(m_new)
        return tuple(new_m), tuple(new_l)

    init = (tuple(jnp.full((1, tq), NEG / 2, F32) for _ in range(4)),
            tuple(jnp.zeros((1, tq), F32) for _ in range(4)))
    _, ls = lax.fori_loop(0, nkb, body, init)
    out = jnp.concatenate([acc_ref[h * HEAD_DIM:(h + 1) * HEAD_DIM, :] / ls[h] for h in range(4)], axis=0)
    return out.T.astype(BF16)


def _tie_select(eq, rem, tri):
    parts = []
    for c in range(eq.shape[1] // LANES):
        e = eq[:, c * LANES:(c + 1) * LANES]
        pre = _dot(jnp.where(e, 1.0, 0.0).astype(BF16), tri)
        parts.append(jnp.where(e & (pre <= rem), 1, 0))
        rem = rem - pre[:, LANES - 1:LANES]
    return jnp.concatenate(parts, axis=1), rem


def _dsa_query_prep(iq, smq):
    lane = lax.broadcasted_iota(I32, iq.shape, 1)
    iqh = []
    for h in range(IDX_HEADS):
        r = iq if h == 0 else pltpu.roll(iq, LANES - IDX_DIM * h, 1)
        iqh.append(jnp.where(lane < IDX_DIM, r, 0.0).astype(BF16))
    wcol = [smq[:, SM_IW + h:SM_IW + h + 1] for h in range(IDX_HEADS)]
    return iqh, wcol


def _dsa_scores(iqh, wcol, ik_blk, contract32=False):
    sc = None
    for h in range(IDX_HEADS):
        a = iqh[h][:, :IDX_DIM] if contract32 else iqh[h]
        z = _dot_nt(a, ik_blk)
        t = jnp.maximum(z, 0.0) * wcol[h]
        sc = t if sc is None else sc + t
    return sc


def _dsa_prompt_kernel(qt_ref, iqt_ref, iwt_ref, k_ref, vt_ref, smk_ref, o_ref,
                       sc_ref, acc_ref, rem_ref, *, tq, tk, topk):
    j = pl.program_id(1)
    nkb = j + 1
    q0 = j * tq
    iqt = iqt_ref[...]
    pad = jnp.zeros((LANES - IDX_DIM, tq), BF16)
    iqh = [jnp.concatenate([iqt[IDX_DIM * h:IDX_DIM * (h + 1), :], pad], axis=0) for h in range(IDX_HEADS)]
    iw = iwt_ref[...]
    row = lax.broadcasted_iota(I32, (tk, tq), 0)
    col = lax.broadcasted_iota(I32, (tk, tq), 1)
    qchunk = (q0 + col) >> 6

    def visible(off):
        return ((off + row) >> 6) <= qchunk

    def score_body(kb, carry):
        off = pl.multiple_of(kb * tk, tk)
        ik = smk_ref[pl.ds(off, tk), :].astype(BF16)
        sc = None
        for h in range(IDX_HEADS):
            t = jnp.maximum(_dot(ik, iqh[h]), 0.0) * iw[h:h + 1, :]
            sc = t if sc is None else sc + t
        sc_ref[pl.ds(off, tk), :] = jnp.where(visible(off), sc, -jnp.inf)
        return carry

    lax.fori_loop(0, nkb, score_body, 0)

    def count_cmp(trial, strict):
        def body(kb, c):
            off = pl.multiple_of(kb * tk, tk)
            s = sc_ref[pl.ds(off, tk), :]
            hit = (s > trial) if strict else (s >= trial)
            return c + jnp.sum(jnp.where(hit, 1, 0), axis=0, keepdims=True)
        return lax.fori_loop(0, nkb, body, jnp.zeros((1, tq), I32))

    thr = _kth_largest(lambda t: count_cmp(t, False), (1, tq), topk)
    rem_ref[0:1, :] = (topk - count_cmp(thr, True)).astype(F32)
    qh = _split_heads_qt(qt_ref[...])
    tri = _tri(tk)

    def sel_fn(off):
        s = sc_ref[pl.ds(off, tk), :]
        eq = s == thr
        pre = _dot(tri, jnp.where(eq, 1.0, 0.0).astype(BF16))
        rem = rem_ref[0:1, :]
        keep = eq & (pre <= rem)
        rem_ref[0:1, :] = rem - pre[tk - 1:tk, :]
        return visible(off) & ((s > thr) | keep)

    def logits_fn(off):
        k_blk = k_ref[pl.ds(off, tk), :]
        return [_dot(k_blk, qh[h]) for h in range(A_HEADS)]

    o_ref[...] = _flash_t(nkb, tk, tq, logits_fn, sel_fn, vt_ref, acc_ref)


def _dsa_prompt(aqt, iqt, iwt, akb, avt, sm, *, batch, seq, tq, topk, n_out):
    nq = seq // tq
    tk = tq
    qcol = lambda w: pl.BlockSpec((w, tq), lambda b, j: (0, b * nq + j))
    krow = lambda w: pl.BlockSpec((seq, w), lambda b, j: (b, 0))
    kern = functools.partial(_dsa_prompt_kernel, tq=tq, tk=tk, topk=topk)
    return pl.pallas_call(
        kern,
        grid=(batch, nq),
        in_specs=[qcol(HW), qcol(LANES), qcol(8), krow(HW),
                  pl.BlockSpec((HW, seq), lambda b, j: (0, b)), krow(LANES)],
        out_specs=pl.BlockSpec((tq, HW), lambda b, j: (b * nq + j, 0)),
        out_shape=jax.ShapeDtypeStruct((n_out, HW), BF16),
        scratch_shapes=[pltpu.VMEM((seq, tq), F32), pltpu.VMEM((HW, tq), F32), pltpu.VMEM((8, tq), F32)],
        compiler_params=_cparams(("parallel", "arbitrary")),
        name="dsa_prompt",
    )(aqt, iqt, iwt, akb, avt, sm)


def _dsa_sample_kernel(q_ref, iq_ref, smq_ref, kn_ref, vn_ref, ck_ref, cv_ref, cik_ref, prev_ref, o_ref,
                       *, tq, past, topk, nb):
    pad = LANES - tq
    lk = past + LANES
    lane = lax.broadcasted_iota(I32, (tq, LANES), 1)
    col = lax.broadcasted_iota(I32, (tq, lk), 1)
    vis = col < past + tq
    scs = []
    for s in range(nb):
        rows = slice(s * tq, (s + 1) * tq)
        smq = smq_ref[rows, :]
        iqh, wcol = _dsa_query_prep(iq_ref[rows, :], smq)
        sc_c = _dsa_scores(iqh, wcol, cik_ref[s].astype(BF16), contract32=True)
        ik_new = jnp.concatenate([smq.astype(BF16), jnp.zeros((pad, LANES), BF16)], axis=0)
        sc_n = _dsa_scores(iqh, wcol, ik_new)
        sc_n = jnp.where(lane < tq, sc_n, -jnp.inf)
        scs.append(jnp.concatenate([sc_c, sc_n], axis=1))

    def count_ge(trials):
        return tuple(jnp.sum(jnp.where(sc >= t, 1, 0), axis=1, keepdims=True) for sc, t in zip(scs, trials))

    thr_all = _kth_largest_multi(count_ge, (tq, 1), topk, nb)
    tri = _tri128()
    for s in range(nb):
        rows = slice(s * tq, (s + 1) * tq)
        sc = scs[s]
        thr = thr_all[s]
        n_gt = jnp.sum(jnp.where(sc > thr, 1, 0), axis=1, keepdims=True)
        need = (topk - n_gt).astype(F32)
        tie, _ = _tie_select(sc == thr, need, tri)
        sel = vis & ((sc > thr) | (tie > 0))
        k_all = jnp.concatenate([ck_ref[s].astype(BF16), kn_ref[rows, :], jnp.zeros((pad, HW), BF16)], axis=0)
        v_all = jnp.concatenate([cv_ref[s].astype(BF16), vn_ref[rows, :], jnp.zeros((pad, HW), BF16)], axis=0)
        qh = _split_heads_q(q_ref[rows, :])
        o_ref[rows, :] = _direct_attend([_dot_nt(qh[h], k_all) for h in range(A_HEADS)], sel, v_all)


def _dsa_sample(aq, iq, sm, akb, avb, cache_k, cache_v, cache_ik, o_prev, *, n_prompt, dbatch, dseq, topk):
    past = cache_k.shape[1]
    nb = 4 if dbatch % 4 == 0 else 1
    base = n_prompt // (nb * dseq)
    qrow = lambda w: pl.BlockSpec((nb * dseq, w), lambda b: (base + b, 0))
    crow = lambda w: pl.BlockSpec((nb, past, w), lambda b: (b, 0, 0))
    kern = functools.partial(_dsa_sample_kernel, tq=dseq, past=past, topk=topk, nb=nb)
    return pl.pallas_call(
        kern,
        grid=(dbatch // nb,),
        in_specs=[qrow(HW), qrow(LANES), qrow(LANES), qrow(HW), qrow(HW),
                  crow(HW), crow(HW), crow(IDX_DIM), pl.BlockSpec(memory_space=pl.ANY)],
        out_specs=qrow(HW),
        out_shape=jax.ShapeDtypeStruct(o_prev.shape, BF16),
        input_output_aliases={8: 0},
        compiler_params=_cparams(("parallel",)),
        name="dsa_sample",
    )(aq, iq, sm, akb, avb, cache_k, cache_v, cache_ik, o_prev)


def _split3(x):
    hi = x.astype(BF16)
    r1 = x - hi.astype(F32)
    mid = r1.astype(BF16)
    lo = (r1 - mid.astype(F32)).astype(BF16)
    return hi, mid, lo


def _fox_prompt_kernel(qt_ref, k_ref, vt_ref, smk_ref, o_ref, ck3_ref, acc_ref, *, tq, tk, seq):
    j = pl.program_id(1)
    tri = _tri(tk)

    @pl.when(j == 0)
    def _():
        lane = lax.broadcasted_iota(I32, (tk, LANES), 1)
        is_f = (lane >= SM_BF) & (lane < SM_BF + B_HEADS)
        carry = jnp.zeros((1, LANES), F32)
        for r in range(seq // tk):
            lf = jnp.where(is_f, smk_ref[r * tk:(r + 1) * tk, :], 0.0)
            cs = carry
            for piece in _split3(lf):
                cs = cs + _dot(tri, piece)
            carry = cs[tk - 1:tk, :]
            nb = -cs
            hi = nb.astype(BF16).astype(F32)
            r1 = nb - hi
            mid = r1.astype(BF16).astype(F32)
            lo = r1 - mid
            slab = jnp.where(is_f, hi, pltpu.roll(mid, B_HEADS, 1) + pltpu.roll(lo, 2 * B_HEADS, 1))
            ck3_ref[r * tk:(r + 1) * tk, :] = slab.astype(BF16)

    q0 = j * tq
    row = lax.broadcasted_iota(I32, (tk, tq), 0)
    col = lax.broadcasted_iota(I32, (tk, tq), 1)
    qpos = q0 + col
    qh = _split_heads_qt(qt_ref[...])
    srow = lax.broadcasted_iota(I32, (LANES, tq), 0)
    pick = [jnp.where((srow == SM_BF + h) | (srow == SM_BF + B_HEADS + h) | (srow == SM_BF + 2 * B_HEADS + h),
                      1.0, 0.0).astype(BF16) for h in range(B_HEADS)]

    def sel_fn(off):
        return (off + row) <= qpos

    def logits_fn(off):
        k_blk = k_ref[pl.ds(off, tk), :]
        ck3 = ck3_ref[pl.ds(off, tk), :]
        return [_dot(k_blk, qh[h]) + _dot(ck3, pick[h]) for h in range(B_HEADS)]

    o_ref[...] = _flash_t(j + 1, tk, tq, logits_fn, sel_fn, vt_ref, acc_ref)


def _fox_prompt(bqt, bkb, bvt, sm, *, batch, seq, tq, n_out):
    nq = seq // tq
    tk = tq
    krow = lambda w: pl.BlockSpec((seq, w), lambda b, j: (b, 0))
    kern = functools.partial(_fox_prompt_kernel, tq=tq, tk=tk, seq=seq)
    return pl.pallas_call(
        kern,
        grid=(batch, nq),
        in_specs=[pl.BlockSpec((HW, tq), lambda b, j: (0, b * nq + j)), krow(HW),
                  pl.BlockSpec((HW, seq), lambda b, j: (0, b)), krow(LANES)],
        out_specs=pl.BlockSpec((tq, HW), lambda b, j: (b * nq + j, 0)),
        out_shape=jax.ShapeDtypeStruct((n_out, HW), BF16),
        scratch_shapes=[pltpu.VMEM((seq, LANES), BF16), pltpu.VMEM((HW, tq), F32)],
        compiler_params=_cparams(("parallel", "arbitrary")),
        name="fox_prompt",
    )(bqt, bkb, bvt, sm)


def _fox_sample_kernel(q_ref, kn_ref, vn_ref, ck_ref, cv_ref, clf_ref, nlf_ref, prev_ref, o_ref, *, tq, past):
    pad = LANES - tq
    cum_c = _lane_cumsum(clf_ref[0])
    cum_n = _lane_cumsum(nlf_ref[0])
    bias_c = cum_c[:, past - 1:past] - cum_c
    bias_n = -cum_n
    bias = jnp.concatenate([bias_c, bias_n], axis=1)
    lk = past + LANES
    row = lax.broadcasted_iota(I32, (tq, lk), 0)
    col = lax.broadcasted_iota(I32, (tq, lk), 1)
    sel = col <= past + row
    k_all = jnp.concatenate([ck_ref[0].astype(BF16), kn_ref[...], jnp.zeros((pad, HW), BF16)], axis=0)
    v_all = jnp.concatenate([cv_ref[0].astype(BF16), vn_ref[...], jnp.zeros((pad, HW), BF16)], axis=0)
    qh = _split_heads_q(q_ref[...])
    logits = [_dot_nt(qh[h], k_all) + bias[h:h + 1, :] for h in range(B_HEADS)]
    o_ref[...] = _direct_attend(logits, sel, v_all)


def _fox_sample(bq, bkb, bvb, cache_k, cache_v, cache_lft, new_lft, o_prev, *, n_prompt, dbatch, dseq):
    past = cache_k.shape[1]
    base = n_prompt // dseq
    qrow = lambda w: pl.BlockSpec((dseq, w), lambda b: (base + b, 0))
    crow = lambda w: pl.BlockSpec((1, past, w), lambda b: (b, 0, 0))
    kern = functools.partial(_fox_sample_kernel, tq=dseq, past=past)
    return pl.pallas_call(
        kern,
        grid=(dbatch,),
        in_specs=[qrow(HW), qrow(HW), qrow(HW), crow(HW), crow(HW),
                  pl.BlockSpec((1, 8, past), lambda b: (b, 0, 0)),
                  pl.BlockSpec((1, 8, LANES), lambda b: (b, 0, 0)),
                  pl.BlockSpec(memory_space=pl.ANY)],
        out_specs=qrow(HW),
        out_shape=jax.ShapeDtypeStruct(o_prev.shape, BF16),
        input_output_aliases={7: 0},
        compiler_params=_cparams(("parallel",)),
        name="fox_sample",
    )(bq, bkb, bvb, cache_k, cache_v, cache_lft, new_lft, o_prev)


def _mla_logits(q, kfull_blk):
    return [_dot_nt(q[:, h * LANES:(h + 1) * LANES], kfull_blk[:, h * LANES:(h + 1) * LANES])
            for h in range(C_HEADS)]


def _mla_prompt_kernel(qt_ref, ckv_ref, ckvt_ref, smk_ref, w1_ref, w2_ref, w1vt_ref, o_ref,
                       kf_ref, vt_ref, acc_ref, *, tq, tk, seq):
    j = pl.program_id(1)

    @pl.when(j == 0)
    def _():
        rc = 512 if seq % 512 == 0 else tk
        for r in range(seq // rc):
            kf = (_dot(ckv_ref[r * rc:(r + 1) * rc, :].astype(BF16), w1_ref[:, 0:512])
                  + _dot(smk_ref[r * rc:(r + 1) * rc, :].astype(BF16), w2_ref[:, 0:512]))
            kf_ref[r * rc:(r + 1) * rc, :] = kf.astype(BF16)
            vt_ref[:, r * rc:(r + 1) * rc] = _dot(w1vt_ref[...], ckvt_ref[:, r * rc:(r + 1) * rc]).astype(BF16)

    q0 = j * tq
    row = lax.broadcasted_iota(I32, (tk, tq), 0)
    col = lax.broadcasted_iota(I32, (tk, tq), 1)
    qchunk = (q0 + col) >> 6
    qt = qt_ref[...]

    def sel_fn(off):
        return ((off + row) >> 6) <= qchunk

    def logits_fn(off):
        kf = kf_ref[pl.ds(off, tk), :]
        return [_dot(kf[:, h * LANES:(h + 1) * LANES], qt[h * LANES:(h + 1) * LANES, :])
                for h in range(C_HEADS)]

    o_ref[...] = _flash_t(j + 1, tk, tq, logits_fn, sel_fn, vt_ref, acc_ref)


def _mla_prompt(qct, ckv, ckvt, sm, w1, w2, w1vt, *, batch, seq, tq, n_out):
    nq = seq // tq
    tk = tq
    krow = lambda w: pl.BlockSpec((seq, w), lambda b, j: (b, 0))
    full = lambda a: pl.BlockSpec(a.shape, lambda b, j: (0,) * a.ndim)
    kern = functools.partial(_mla_prompt_kernel, tq=tq, tk=tk, seq=seq)
    return pl.pallas_call(
        kern,
        grid=(batch, nq),
        in_specs=[pl.BlockSpec((512, tq), lambda b, j: (0, b * nq + j)), krow(LANES),
                  pl.BlockSpec((LANES, seq), lambda b, j: (0, b)), krow(LANES),
                  full(w1), full(w2), full(w1vt)],
        out_specs=pl.BlockSpec((tq, HW), lambda b, j: (b * nq + j, 0)),
        out_shape=jax.ShapeDtypeStruct((n_out, HW), BF16),
        scratch_shapes=[pltpu.VMEM((seq, 512), BF16), pltpu.VMEM((HW, seq), BF16),
                        pltpu.VMEM((HW, tq), F32)],
        compiler_params=_cparams(("parallel", "arbitrary")),
        name="mla_prompt",
    )(qct, ckv, ckvt, sm, w1, w2, w1vt)


def _mla_sample_kernel(q_ref, ckvn_ref, smn_ref, cckv_ref, ckr_ref, w1_ref, w2_ref, w2c_ref, prev_ref, o_ref,
                       *, tq, past):
    pad = LANES - tq
    kv_c = _dot(cckv_ref[0].astype(BF16), w1_ref[...]) + _dot(ckr_ref[0].astype(BF16), w2c_ref[...])
    kv_n = _dot(ckvn_ref[...].astype(BF16), w1_ref[...]) + _dot(smn_ref[...].astype(BF16), w2_ref[...])
    kv = jnp.concatenate([kv_c, kv_n, jnp.zeros((pad, 768), F32)], axis=0).astype(BF16)
    lk = past + LANES
    col = lax.broadcasted_iota(I32, (tq, lk), 1)
    sel = col < past + tq
    o_ref[...] = _direct_attend(_mla_logits(q_ref[...], kv[:, :512]), sel, kv[:, 512:])


def _mla_sample(qc, ckv, sm, cache_ckv, cache_kr, w1, w2, w2c, o_prev, *, n_prompt, dbatch, dseq):
    past = cache_ckv.shape[1]
    base = n_prompt // dseq
    qrow = lambda w: pl.BlockSpec((dseq, w), lambda b: (base + b, 0))
    crow = lambda w: pl.BlockSpec((1, past, w), lambda b: (b, 0, 0))
    full = lambda a: pl.BlockSpec(a.shape, lambda b: (0,) * a.ndim)
    kern = functools.partial(_mla_sample_kernel, tq=dseq, past=past)
    return pl.pallas_call(
        kern,
        grid=(dbatch,),
        in_specs=[qrow(512), qrow(LANES), qrow(LANES), crow(C_KV_RANK), crow(C_ROPE),
                  full(w1), full(w2), full(w2c), pl.BlockSpec(memory_space=pl.ANY)],
        out_specs=qrow(HW),
        out_shape=jax.ShapeDtypeStruct(o_prev.shape, BF16),
        input_output_aliases={8: 0},
        compiler_params=_cparams(("parallel",)),
        name="mla_sample",
    )(qc, ckv, sm, cache_ckv, cache_kr, w1, w2, w2c, o_prev)


def _merge_kernel(x_ref, g1_ref, oa_ref, ob_ref, oc_ref, wg_ref, wua_ref, wub_ref, wuc_ref, wout_ref,
                  xo_ref):
    x = x_ref[...]
    d = x.shape[1]
    h = _rmsnorm(x, g1_ref[...]).astype(BF16)
    gates = jax.nn.sigmoid(_dot(h, wg_ref[...]))
    merged = (gates[:, 0:d] * _dot(oa_ref[...], wua_ref[...])
              + gates[:, d:2 * d] * _dot(ob_ref[...], wub_ref[...])
              + gates[:, 2 * d:3 * d] * _dot(oc_ref[...], wuc_ref[...]))
    xo_ref[...] = x + _dot(merged.astype(BF16), wout_ref[...])


def _merge(x, g1, oa, ob, oc, wg, wua, wub, wuc, wout, *, tm):
    n, d = x.shape
    row = lambda w: pl.BlockSpec((tm, w), lambda i: (i, 0))
    full = lambda a: pl.BlockSpec(a.shape, lambda i: (0,) * a.ndim)
    return pl.pallas_call(
        _merge_kernel,
        grid=(n // tm,),
        in_specs=[row(d), full(g1), row(HW), row(HW), row(HW),
                  full(wg), full(wua), full(wub), full(wuc), full(wout)],
        out_specs=row(d),
        out_shape=jax.ShapeDtypeStruct((n, d), F32),
        compiler_params=_cparams(("parallel",)),
        name="merge",
    )(x, g1, oa, ob, oc, wg, wua, wub, wuc, wout)


WORD = I32


def _row_words(x):
    u = pltpu.bitcast(x.astype(BF16).astype(F32), I32)
    return u | lax.shift_right_logical(u, 16)


def _bcast_rows_bf16(words_row, rows):
    return pltpu.bitcast(jnp.broadcast_to(words_row, (rows // 2, words_row.shape[1])), BF16)


def _top_values(s, k, with_rank=False):
    vals = []
    rank = jnp.full(s.shape, float(k), F32)
    for r in range(k):
        m = jnp.max(s, axis=0, keepdims=True)
        vals.append(m)
        hit = s == m
        if with_rank:
            rank = jnp.where(hit, float(r), rank)
        s = jnp.where(hit, -jnp.inf, s)
    return (vals, rank) if with_rank else vals


def _peer_kernel(x_ref, g2_ref, wqt_ref, k1_ref, k2_ref, u_ref, vt_ref, o_ref,
                 ht_ref, qt_ref, acc_ref, rank_ref, e2_ref, cnt_ref, c_ref, v2_ref, cand_ref,
                 *, nk, na, grp):
    k = pl.program_id(1)
    tm = x_ref.shape[0]
    half = PEER_QDIM // 2

    @pl.when(k == 0)
    def _():
        h = _rmsnorm(x_ref[...], g2_ref[...])
        ht = h.T.astype(BF16)
        ht_ref[...] = ht
        qt_ref[...] = _dot(wqt_ref[...], ht).astype(BF16)
        acc_ref[...] = jnp.zeros(acc_ref.shape, F32)

        def select_body(hh, carry):
            qrow = pl.multiple_of(hh * PEER_QDIM, PEER_QDIM)
            krow = pl.multiple_of(hh * nk, nk)
            q1 = qt_ref[pl.ds(qrow, half), :]
            q2 = qt_ref[pl.ds(qrow + half, half), :]
            s1 = _dot(k1_ref[pl.ds(krow, nk), :], q1)
            s2 = _dot(k2_ref[pl.ds(krow, nk), :], q2)
            v1 = _top_values(s1, PEER_TOPK)
            v2, rank2 = _top_values(s2, PEER_TOPK, with_rank=True)
            for jj in range(PEER_TOPK):
                v2_ref[jj:jj + 1, :] = v2[jj]
            cand_ref[...] = jnp.full(cand_ref.shape, -jnp.inf, F32)
            off = 0
            for i in range(PEER_TOPK):
                nj = PEER_TOPK // (i + 1)
                cand_ref[off:off + nj, :] = v1[i] + v2_ref[0:nj, :]
                off += nj
            cand = cand_ref[...]
            tau = _top_values(cand, PEER_TOPK)[-1]
            top = v1[0] + v2[0]
            z = jnp.sum(jnp.where(cand >= tau, jnp.exp(cand - top), 0.0), axis=0, keepdims=True)
            cnt = jnp.zeros((nk, tm), F32)
            for jj in range(PEER_TOPK):
                cnt = jnp.where((s1 + v2[jj]) >= tau, float(jj + 1), cnt)
            rank_ref[hh] = rank2.astype(BF16)
            e2_ref[hh] = jnp.exp(s2 - v2[0]).astype(BF16)
            cnt_ref[hh] = _row_words(cnt)
            c_ref[hh] = _row_words(jnp.exp(s1 - v1[0]) / z)
            return carry

        lax.fori_loop(0, PEER_HEADS, select_body, 0)

    a0 = pl.multiple_of(k * na, na)
    cnt_rows = [cnt_ref[hh, pl.ds(a0, na), :] for hh in range(PEER_HEADS)]
    c_rows = [c_ref[hh, pl.ds(a0, na), :] for hh in range(PEER_HEADS)]
    zero = jnp.zeros((nk, tm), BF16)
    total = None
    for p in range(na // grp):
        rows = slice(p * grp * nk, (p + 1) * grp * nk)
        act = jax.nn.gelu(_dot(u_ref[rows, :], ht_ref[...]).astype(BF16))
        gated = []
        for ai in range(p * grp, (p + 1) * grp):
            w = None
            for hh in range(PEER_HEADS):
                cb = _bcast_rows_bf16(cnt_rows[hh][ai:ai + 1, :], nk)
                cc = _bcast_rows_bf16(c_rows[hh][ai:ai + 1, :], nk)
                t = jnp.where(rank_ref[hh] < cb, e2_ref[hh], zero) * cc
                w = t if w is None else w + t
            r0 = (ai - p * grp) * nk
            gated.append(w * act[r0:r0 + nk, :])
        part = _dot(vt_ref[:, rows], jnp.concatenate(gated, axis=0))
        total = part if total is None else total + part
    acc_ref[...] += total

    @pl.when(k == pl.num_programs(1) - 1)
    def _():
        o_ref[...] = x_ref[...] + acc_ref[...].T


def _peer(x, g2, wqt, k1, k2, u, vt, *, tm, na):
    n, d = x.shape
    nk = k1.shape[0] // PEER_HEADS
    te = na * nk
    nchunks = nk // na
    ncand = sum(PEER_TOPK // (i + 1) for i in range(PEER_TOPK))
    ncand = -(-ncand // 8) * 8
    full = lambda a: pl.BlockSpec(a.shape, lambda i, k: (0,) * a.ndim)
    kern = functools.partial(_peer_kernel, nk=nk, na=na, grp=na)
    return pl.pallas_call(
        kern,
        grid=(n // tm, nchunks),
        in_specs=[pl.BlockSpec((tm, d), lambda i, k: (i, 0)), full(g2), full(wqt), full(k1), full(k2),
                  pl.BlockSpec((te, d), lambda i, k: (k, 0)),
                  pl.BlockSpec((None, d, te), lambda i, k: (k, 0, 0))],
        out_specs=pl.BlockSpec((tm, d), lambda i, k: (i, 0)),
        out_shape=jax.ShapeDtypeStruct((n, d), F32),
        scratch_shapes=[pltpu.VMEM((d, tm), BF16), pltpu.VMEM((PEER_HEADS * PEER_QDIM, tm), BF16),
                        pltpu.VMEM((d, tm), F32),
                        pltpu.VMEM((PEER_HEADS, nk, tm), BF16), pltpu.VMEM((PEER_HEADS, nk, tm), BF16),
                        pltpu.VMEM((PEER_HEADS, nk, tm), WORD), pltpu.VMEM((PEER_HEADS, nk, tm), WORD),
                        pltpu.VMEM((PEER_TOPK, tm), F32), pltpu.VMEM((ncand, tm), F32)],
        compiler_params=_cparams(("parallel", "arbitrary")),
        name="peer",
    )(x, g2, wqt, k1, k2, u, vt)


def _final_norm_kernel(x_ref, g_ref, op_ref, os_ref, *, npt):
    y = _rmsnorm(x_ref[...], g_ref[...])
    i = pl.program_id(0)

    @pl.when(i < npt)
    def _():
        op_ref[...] = y

    @pl.when(i >= npt)
    def _():
        os_ref[...] = y


def _final_norm(x, g, *, tm, n_prompt):
    n, d = x.shape
    npt = n_prompt // tm
    return pl.pallas_call(
        functools.partial(_final_norm_kernel, npt=npt),
        grid=(n // tm,),
        in_specs=[pl.BlockSpec((tm, d), lambda i: (i, 0)), pl.BlockSpec((1, d), lambda i: (0, 0))],
        out_specs=[pl.BlockSpec((tm, d), lambda i: (jnp.minimum(i, npt - 1), 0)),
                   pl.BlockSpec((tm, d), lambda i: (jnp.maximum(i - npt, 0), 0))],
        out_shape=[jax.ShapeDtypeStruct((n_prompt, d), F32), jax.ShapeDtypeStruct((n - n_prompt, d), F32)],
        compiler_params=_cparams(("arbitrary",)),
        name="final_norm",
    )(x, g)


def _rope_tables(pos):
    rows = pos.shape[0]

    def cs(half):
        inv = jnp.power(ROPE_THETA, -jnp.arange(half, dtype=F32) / half)
        ang = pos.astype(F32)[:, None] * inv[None, :]
        c, s = jnp.cos(ang), jnp.sin(ang)
        return jnp.concatenate([c, c], axis=1), jnp.concatenate([-s, s], axis=1)

    c64, s64 = cs(HEAD_DIM // 2)
    c32, s32 = cs(IDX_DIM // 2)
    one32 = jnp.ones((rows, 32), F32)
    zero32 = jnp.zeros((rows, 32), F32)
    c_sm = jnp.concatenate([c32, c32, one32, one32], axis=1)
    s_sm = jnp.concatenate([s32, s32, zero32, zero32], axis=1)
    c_qc = jnp.concatenate([one32, one32, c32, one32], axis=1)
    s_qc = jnp.concatenate([zero32, zero32, s32, zero32], axis=1)
    ctab = jnp.concatenate([jnp.tile(c64, (1, 4)), jnp.tile(c32, (1, 4)), c_sm, jnp.tile(c_qc, (1, 4))], axis=1)
    stab = jnp.concatenate([jnp.tile(s64, (1, 4)), jnp.tile(s32, (1, 4)), s_sm, jnp.tile(s_qc, (1, 4))], axis=1)
    return ctab, stab


def _split_w_in(w_in):
    sizes = (HW, HW, HW, IDX_HEADS * IDX_DIM, IDX_DIM, IDX_HEADS, HW, HW, HW, B_HEADS,
             C_Q_RANK, C_KV_RANK, C_ROPE)
    offs = np.cumsum((0,) + sizes)
    parts = [w_in[..., int(offs[i]):int(offs[i + 1])] for i in range(len(sizes))]
    gates = w_in[..., int(offs[-1]):]
    return parts, gates


def _pack_weights(w_in, w_uq, w_ukv, b_forget):
    depth, d, _ = w_in.shape
    (a_q, a_k, a_v, i_q, i_k, i_w, b_q, b_k, b_v, b_f, c_q, c_kv, c_kr), gates = _split_w_in(w_in)
    z = lambda w: jnp.zeros((depth, d, w), F32)
    small = jnp.concatenate([i_k, c_kr, b_f, z(4), i_w, z(LANES - SM_IW - IDX_HEADS)], axis=-1)
    wmain = jnp.concatenate([a_q, a_k, a_v, b_q, b_k, b_v, c_q, c_kv, i_q, small], axis=-1).astype(BF16)
    wg = gates.astype(BF16)

    uq = w_uq.reshape(depth, C_Q_RANK, C_HEADS, C_NOPE + C_ROPE)
    uq = jnp.concatenate([uq, jnp.zeros((depth, C_Q_RANK, C_HEADS, 32), F32)], axis=-1)
    wuq = uq.reshape(depth, C_Q_RANK, C_HEADS * LANES).astype(BF16)

    ukv = w_ukv.reshape(depth, C_KV_RANK, C_HEADS, C_NOPE + C_VDIM)
    kn = jnp.concatenate([ukv[..., :C_NOPE], jnp.zeros((depth, C_KV_RANK, C_HEADS, 64), F32)], axis=-1)
    w1 = jnp.concatenate([kn.reshape(depth, C_KV_RANK, 512),
                          ukv[..., C_NOPE:].reshape(depth, C_KV_RANK, HW)], axis=-1).astype(BF16)
    w1vt = jnp.swapaxes(w1[:, :, 512:], 1, 2)
    eye = np.zeros((C_ROPE, 768), np.float32)
    for h in range(C_HEADS):
        for r in range(C_ROPE):
            eye[r, h * LANES + C_NOPE + r] = 1.0
    w2c = jnp.asarray(eye).astype(BF16)
    w2_np = np.zeros((LANES, 768), np.float32)
    w2_np[SM_KR:SM_KR + C_ROPE] = eye
    w2 = jnp.asarray(w2_np).astype(BF16)

    aux = jnp.zeros((depth, 8, LANES), F32)
    aux = aux.at[:, 0, SM_BF:SM_BF + B_HEADS].set(b_forget)
    scale = np.ones((LANES,), np.float32)
    scale[SM_IW:SM_IW + IDX_HEADS] = (IDX_HEADS * IDX_DIM) ** -0.5
    aux = aux.at[:, 1, :].set(jnp.asarray(scale))
    return wmain, wg, wuq, w1, w1vt, w2, w2c, aux


def _tile_rows(n):
    for t in (512, 256, 128):
        if n % t == 0:
            return t
    raise ValueError(f"token count {n} is not a multiple of 128")


def kernel(x_prompt, x_sample, cache_k_a, cache_v_a, cache_kidx_a, cache_k_b, cache_v_b, cache_logf_b,
           cache_ckv_c, cache_krope_c, norm1_g, w_in, c_q_norm_g, c_kv_norm_g, w_uq, w_ukv, b_forget,
           w_up_a, w_up_b, w_up_c, w_out, norm2_g, peer_w_q, peer_keys, peer_u, peer_v, final_norm_g):
    batch, seq, d = x_prompt.shape
    dbatch, dseq, _ = x_sample.shape
    depth = w_in.shape[0]
    past = cache_k_a.shape[2]
    n_p = batch * seq
    n_s = dbatch * dseq
    n = n_p + n_s
    nk = peer_keys.shape[3]
    assert dseq == CHUNK and past % LANES == 0 and seq % 256 == 0

    tm = min(_tile_rows(seq), _tile_rows(n_s))
    tq = 512 if seq % 512 == 0 else 256
    topk_p = min(TOPK_MAX, seq // 4)
    topk_s = min(TOPK_MAX, (past + dseq) // 4)

    pos = jnp.concatenate([jnp.arange(seq, dtype=jnp.int32),
                           past + (jnp.arange(tm, dtype=jnp.int32) % dseq)])
    ctab, stab = _rope_tables(pos)

    wmain, wg, wuq, w1, w1vt, w2, w2c, aux = _pack_weights(w_in, w_uq, w_ukv, b_forget)
    wua, wub, wuc, wout = (w.astype(BF16) for w in (w_up_a, w_up_b, w_up_c, w_out))
    wqt = jnp.swapaxes(peer_w_q, 1, 2).astype(BF16)
    half = PEER_QDIM // 2
    k1 = peer_keys[:, :, 0].reshape(depth, PEER_HEADS * nk, half).astype(BF16)
    k2 = peer_keys[:, :, 1].reshape(depth, PEER_HEADS * nk, half).astype(BF16)
    pu = peer_u.astype(BF16)
    na = 8 if nk % 8 == 0 else nk
    pvt = peer_v.astype(BF16).reshape(depth, nk // na, na * nk, d).transpose(0, 1, 3, 2)
    clf_t = jnp.swapaxes(cache_logf_b, 2, 3)
    clf_t = jnp.concatenate([clf_t, jnp.zeros_like(clf_t)], axis=2)

    g_row = lambda g: g.reshape(1, -1)
    x = jnp.concatenate([x_prompt.reshape(n_p, d), x_sample.reshape(n_s, d)], axis=0)
    nr = len(ROW_WIDTHS)
    rows_p = [[] for _ in range(nr)]
    rows_s = [[] for _ in range(nr)]

    for l in range(depth):
        g1 = g_row(norm1_g[l])
        outs = _inproj(x, g1, wmain[l], wuq[l], g_row(c_q_norm_g[l]), g_row(c_kv_norm_g[l]), aux[l],
                       ctab, stab, tm=tm, n_prompt=n_p, seq=seq)
        (aq, akb, avb, bq, bkb, bvb, qc, ckv, iq, sm, lft,
         aqt, avt, bqt, bvt, qct, iqt, ckvt, iwt) = outs[:19]
        for i in range(nr):
            rows_p[i].append(outs[19 + i])
            rows_s[i].append(outs[19 + nr + i])

        new_lft = lft[:, n_p:].reshape(8, dbatch, dseq).transpose(1, 0, 2)
        new_lft = jnp.concatenate([new_lft, jnp.zeros((dbatch, 8, LANES - dseq), F32)], axis=2)

        oa = _dsa_prompt(aqt, iqt, iwt, akb, avt, sm, batch=batch, seq=seq, tq=tq, topk=topk_p, n_out=n)
        oa = _dsa_sample(aq, iq, sm, akb, avb,
                         cache_k_a[l].reshape(dbatch, past, HW), cache_v_a[l].reshape(dbatch, past, HW),
                         cache_kidx_a[l], oa, n_prompt=n_p, dbatch=dbatch, dseq=dseq, topk=topk_s)
        ob = _fox_prompt(bqt, bkb, bvt, sm, batch=batch, seq=seq, tq=tq, n_out=n)
        ob = _fox_sample(bq, bkb, bvb,
                         cache_k_b[l].reshape(dbatch, past, HW), cache_v_b[l].reshape(dbatch, past, HW),
                         clf_t[l], new_lft, ob, n_prompt=n_p, dbatch=dbatch, dseq=dseq)
        oc = _mla_prompt(qct, ckv, ckvt, sm, w1[l], w2, w1vt[l], batch=batch, seq=seq, tq=tq, n_out=n)
        oc = _mla_sample(qc, ckv, sm, cache_ckv_c[l], cache_krope_c[l], w1[l], w2, w2c, oc,
                         n_prompt=n_p, dbatch=dbatch, dseq=dseq)

        x = _merge(x, g1, oa, ob, oc, wg[l], wua[l], wub[l], wuc[l], wout[l], tm=tm)
        x = _peer(x, g_row(norm2_g[l]), wqt[l], k1[l], k2[l], pu[l], pvt[l], tm=tm, na=na)

    y_p, y_s = _final_norm(x, g_row(final_norm_g), tm=tm, n_prompt=n_p)
    y_p = y_p.reshape(batch, seq, d)
    y_s = y_s.reshape(dbatch, dseq, d)

    tails = [(A_HEADS, HEAD_DIM), (A_HEADS, HEAD_DIM), (IDX_DIM,), (B_HEADS, HEAD_DIM), (B_HEADS, HEAD_DIM),
             (B_HEADS,), (C_KV_RANK,), (C_ROPE,)]
    outs_p = [jnp.stack(r).reshape((depth, batch, seq) + t) for r, t in zip(rows_p, tails)]
    outs_s = [jnp.stack(r).reshape((depth, dbatch, dseq) + t) for r, t in zip(rows_s, tails)]
    return (y_p, y_s, *outs_p, *outs_s)
```

```python
import functools

import jax
import jax.numpy as jnp
import numpy as np
from jax import lax
from jax.experimental import pallas as pl
from jax.experimental.pallas import tpu as pltpu

F32 = jnp.float32
BF16 = jnp.bfloat16
I32 = jnp.int32

CHUNK = 64
HEAD_DIM = 64
ROPE_THETA = 10000.0
NORM_EPS = 1e-6
A_HEADS = 4
IDX_HEADS = 4
IDX_DIM = 32
TOPK_MAX = 256
B_HEADS = 4
C_HEADS = 4
C_NOPE = 64
C_ROPE = 32
C_VDIM = 64
C_Q_RANK = 256
C_KV_RANK = 128
N_BRANCHES = 3
PEER_HEADS = 8
PEER_QDIM = 128
PEER_TOPK = 16

HW = A_HEADS * HEAD_DIM
LANES = 128
VMEM_LIMIT = 56 * 1024 * 1024

SM_IK = 0
SM_KR = 32
SM_BF = 64
SM_IW = 72

MC_AQ, MC_AK, MC_AV = 0, 256, 512
MC_BQ, MC_BK, MC_BV = 768, 1024, 1280
MC_CQ = 1536
MC_CKV = 1792
MC_IQ = 1920
MC_SM = 2048
MAIN_COLS = 2176

RT_64 = 0
RT_IQ = 256
RT_SM = 384
RT_QC = 512
RT_COLS = 1024

LOG2E = 1.4426950408889634
NEG = -1e30
KEY_NEG_INF = -2139095041
INT_MIN = -2147483648


def _cparams(sem):
    return pltpu.CompilerParams(dimension_semantics=sem, vmem_limit_bytes=VMEM_LIMIT)


def _rmsnorm(x, g):
    ms = jnp.mean(x * x, axis=-1, keepdims=True)
    return x * lax.rsqrt(ms + NORM_EPS) * g


def _dot(a, b):
    return jnp.dot(a, b, preferred_element_type=F32)


def _dot_nt(a, b):
    return lax.dot_general(a, b, (((1,), (1,)), ((), ())), preferred_element_type=F32)


def _rope(v, c, s, half):
    w = v.shape[-1]
    lane = lax.broadcasted_iota(I32, v.shape, 1)
    first = (lane & (2 * half - 1)) < half
    vr = jnp.where(first, pltpu.roll(v, w - half, 1), pltpu.roll(v, half, 1))
    return v * c + vr * s


def _inproj_kernel(x_ref, g1_ref, wmain_ref, wuq_ref, gq_ref, gkv_ref, aux_ref, ct_ref, st_ref,
                   aq_ref, akb_ref, avb_ref, bq_ref, bkb_ref, bvb_ref,
                   qc_ref, ckv_ref, iq_ref, sm_ref, lft_ref,
                   aqt_ref, avt_ref, bqt_ref, bvt_ref, qct_ref, iqt_ref, ckvt_ref, iwt_ref,
                   *row_refs, npt):
    x = x_ref[...]
    h = _rmsnorm(x, g1_ref[...]).astype(BF16)
    y = _dot(h, wmain_ref[...])
    ct = ct_ref[...]
    st = st_ref[...]

    c64 = ct[:, RT_64:RT_64 + HW]
    s64 = st[:, RT_64:RT_64 + HW]
    aq = _rope(y[:, MC_AQ:MC_AQ + HW], c64, s64, HEAD_DIM // 2)
    aq = aq * (HEAD_DIM ** -0.5 * LOG2E)
    aq_ref[...] = aq.astype(BF16)
    aqt_ref[...] = aq.T.astype(BF16)
    ak = _rope(y[:, MC_AK:MC_AK + HW], c64, s64, HEAD_DIM // 2)
    akb_ref[...] = ak.astype(BF16)
    av = y[:, MC_AV:MC_AV + HW]
    avb_ref[...] = av.astype(BF16)
    avt_ref[...] = av.T.astype(BF16)

    bq = y[:, MC_BQ:MC_BQ + HW] * (HEAD_DIM ** -0.5 * LOG2E)
    bq_ref[...] = bq.astype(BF16)
    bqt_ref[...] = bq.T.astype(BF16)
    bk = y[:, MC_BK:MC_BK + HW]
    bkb_ref[...] = bk.astype(BF16)
    bv = y[:, MC_BV:MC_BV + HW]
    bvb_ref[...] = bv.astype(BF16)
    bvt_ref[...] = bv.T.astype(BF16)

    cq = _rmsnorm(y[:, MC_CQ:MC_CQ + C_Q_RANK], gq_ref[...]).astype(BF16)
    qc = _dot(cq, wuq_ref[...])
    qc = _rope(qc, ct[:, RT_QC:RT_QC + 512], st[:, RT_QC:RT_QC + 512], C_ROPE // 2)
    qc = qc * ((C_NOPE + C_ROPE) ** -0.5 * LOG2E)
    qc_ref[...] = qc.astype(BF16)
    qct_ref[...] = qc.T.astype(BF16)

    ckv = _rmsnorm(y[:, MC_CKV:MC_CKV + C_KV_RANK], gkv_ref[...])
    ckv_ref[...] = ckv
    ckvt_ref[...] = ckv.T.astype(BF16)

    iq = _rope(y[:, MC_IQ:MC_IQ + LANES], ct[:, RT_IQ:RT_IQ + LANES],
               st[:, RT_IQ:RT_IQ + LANES], IDX_DIM // 2)
    iq_ref[...] = iq
    iqt_ref[...] = iq.T.astype(BF16)

    sm = _rope(y[:, MC_SM:MC_SM + LANES], ct[:, RT_SM:RT_SM + LANES],
               st[:, RT_SM:RT_SM + LANES], IDX_DIM // 2)
    lane = lax.broadcasted_iota(I32, sm.shape, 1)
    is_f = (lane >= SM_BF) & (lane < SM_BF + B_HEADS)
    z = sm + aux_ref[0:1, :]
    logsig = jnp.minimum(z, 0.0) - jnp.log1p(jnp.exp(-jnp.abs(z)))
    sm = jnp.where(is_f, logsig, sm * aux_ref[1:2, :])
    sm_ref[...] = sm
    smt = sm.T
    lft_ref[...] = smt[SM_BF:SM_BF + 8, :]
    iwt_ref[...] = smt[SM_IW:SM_IW + 8, :]

    new_rows = (ak, av, sm[:, SM_IK:SM_IK + IDX_DIM], bk, bv, sm[:, SM_BF:SM_BF + B_HEADS],
                ckv, sm[:, SM_KR:SM_KR + C_ROPE])
    nr = len(new_rows)
    i = pl.program_id(0)

    @pl.when(i < npt)
    def _():
        for r, v in zip(row_refs[:nr], new_rows):
            r[...] = v

    @pl.when(i >= npt)
    def _():
        for r, v in zip(row_refs[nr:], new_rows):
            r[...] = v


ROW_WIDTHS = (HW, HW, IDX_DIM, HW, HW, B_HEADS, C_KV_RANK, C_ROPE)


def _inproj(x, g1, wmain, wuq, gq, gkv, aux, ctab, stab, *, tm, n_prompt, seq):
    n, d = x.shape
    nt = n // tm
    npt = n_prompt // tm
    tps = seq // tm

    def tmap(i):
        return (jnp.where(i < npt, i % tps, tps), 0)

    row = lambda w: pl.BlockSpec((tm, w), lambda i: (i, 0))
    prow = lambda w: pl.BlockSpec((tm, w), lambda i: (jnp.minimum(i, npt - 1), 0))
    srow = lambda w: pl.BlockSpec((tm, w), lambda i: (jnp.maximum(i - npt, 0), 0))
    full = lambda a: pl.BlockSpec(a.shape, lambda i: (0,) * a.ndim)
    out_shapes = [
        jax.ShapeDtypeStruct((n, HW), BF16),
        jax.ShapeDtypeStruct((n, HW), BF16),
        jax.ShapeDtypeStruct((n, HW), BF16),
        jax.ShapeDtypeStruct((n, HW), BF16),
        jax.ShapeDtypeStruct((n, HW), BF16),
        jax.ShapeDtypeStruct((n, HW), BF16),
        jax.ShapeDtypeStruct((n, 512), BF16),
        jax.ShapeDtypeStruct((n, LANES), F32),
        jax.ShapeDtypeStruct((n, LANES), F32),
        jax.ShapeDtypeStruct((n, LANES), F32),
        jax.ShapeDtypeStruct((8, n), F32),
        jax.ShapeDtypeStruct((HW, n), BF16),
        jax.ShapeDtypeStruct((HW, n), BF16),
        jax.ShapeDtypeStruct((HW, n), BF16),
        jax.ShapeDtypeStruct((HW, n), BF16),
        jax.ShapeDtypeStruct((512, n), BF16),
        jax.ShapeDtypeStruct((LANES, n), BF16),
        jax.ShapeDtypeStruct((LANES, n), BF16),
        jax.ShapeDtypeStruct((8, n), F32),
    ]
    out_shapes += [jax.ShapeDtypeStruct((n_prompt, w), F32) for w in ROW_WIDTHS]
    out_shapes += [jax.ShapeDtypeStruct((n - n_prompt, w), F32) for w in ROW_WIDTHS]
    col = lambda w: pl.BlockSpec((w, tm), lambda i: (0, i))
    out_specs = [row(HW)] * 6 + [row(512), row(LANES), row(LANES), row(LANES), col(8),
                                 col(HW), col(HW), col(HW), col(HW), col(512), col(LANES), col(LANES),
                                 col(8)]
    out_specs += [prow(w) for w in ROW_WIDTHS] + [srow(w) for w in ROW_WIDTHS]
    return pl.pallas_call(
        functools.partial(_inproj_kernel, npt=npt),
        grid=(nt,),
        in_specs=[row(d), full(g1), full(wmain), full(wuq), full(gq), full(gkv), full(aux),
                  pl.BlockSpec((tm, RT_COLS), tmap), pl.BlockSpec((tm, RT_COLS), tmap)],
        out_specs=out_specs,
        out_shape=out_shapes,
        compiler_params=_cparams(("arbitrary",)),
        name="inproj",
    )(x, g1, wmain, wuq, gq, gkv, aux, ctab, stab)


def _head_lane_mask(shape, h, width=HEAD_DIM):
    lane = lax.broadcasted_iota(I32, shape, 1)
    return (lane >= h * width) & (lane < (h + 1) * width)


def _split_heads_q(q):
    return [jnp.where(_head_lane_mask(q.shape, h), q, jnp.zeros_like(q)) for h in range(4)]


def _direct_attend(logits_list, sel, v_blk):
    out = None
    for h, logits in enumerate(logits_list):
        lm = jnp.where(sel, logits, NEG)
        m = jnp.max(lm, axis=1, keepdims=True)
        p = jnp.where(sel, jnp.exp2(logits - m), 0.0)
        l = jnp.sum(p, axis=1, keepdims=True)
        o = _dot(p.astype(BF16), v_blk) / l
        o = jnp.where(_head_lane_mask(o.shape, h), o, 0.0)
        out = o if out is None else out + o
    return out.astype(BF16)


def _lane_cumsum(x):
    n = x.shape[1]
    lane = lax.broadcasted_iota(I32, x.shape, 1)
    k = 1
    while k < n:
        x = x + jnp.where(lane >= k, pltpu.roll(x, k, 1), 0.0)
        k *= 2
    return x


def _key_to_float(key):
    bits = key ^ ((key >> 31) & 0x7FFFFFFF)
    return jnp.where(key < KEY_NEG_INF, -jnp.inf, pltpu.bitcast(bits, F32))


def _kth_largest(count_ge, shape, k):
    zero = jnp.zeros(shape, I32)
    c0 = count_ge(_key_to_float(zero))
    lo = jnp.where(c0 >= k, zero, jnp.full(shape, INT_MIN, I32))

    def bit_body(i, lo):
        trial = lo + jnp.left_shift(jnp.int32(1), 30 - i)
        c = count_ge(_key_to_float(trial))
        return jnp.where(c >= k, trial, lo)

    lo = lax.fori_loop(0, 31, bit_body, lo)
    return _key_to_float(lo)


def _kth_largest_multi(count_ge, shape, k, parts):
    zero = jnp.zeros(shape, I32)
    c0 = count_ge(tuple(_key_to_float(zero) for _ in range(parts)))
    lo = tuple(jnp.where(c >= k, zero, jnp.full(shape, INT_MIN, I32)) for c in c0)

    def bit_body(i, lo):
        step = jnp.left_shift(jnp.int32(1), 30 - i)
        trial = tuple(l + step for l in lo)
        c = count_ge(tuple(_key_to_float(t) for t in trial))
        return tuple(jnp.where(ci >= k, ti, li) for ci, ti, li in zip(c, trial, lo))

    lo = lax.fori_loop(0, 31, bit_body, lo)
    return tuple(_key_to_float(l) for l in lo)


def _tri128():
    r = lax.broadcasted_iota(I32, (LANES, LANES), 0)
    c = lax.broadcasted_iota(I32, (LANES, LANES), 1)
    return jnp.where(r <= c, 1.0, 0.0).astype(BF16)


def _tri(n):
    r = lax.broadcasted_iota(I32, (n, n), 0)
    c = lax.broadcasted_iota(I32, (n, n), 1)
    return jnp.where(c <= r, 1.0, 0.0).astype(BF16)


def _split_heads_qt(qt):
    row = lax.broadcasted_iota(I32, qt.shape, 0)
    return [jnp.where((row >= h * HEAD_DIM) & (row < (h + 1) * HEAD_DIM), qt, jnp.zeros_like(qt))
            for h in range(4)]


def _flash_t(last, tk, tq, logits_fn, sel_fn, vt_ref, acc_ref):
    acc_ref[...] = jnp.zeros(acc_ref.shape, F32)

    def body(kb, carry, diag):
        ms, ls = carry
        off = pl.multiple_of(kb * tk, tk)
        sel = sel_fn(off, diag)
        lgs = logits_fn(off)
        new_m, new_l = [], []
        for h in range(4):
            lm = lgs[h] if sel is None else jnp.where(sel, lgs[h], NEG)
            m_new = jnp.maximum(ms[h], jnp.max(lm, axis=0, keepdims=True))
            p = jnp.exp2(lm - m_new)
            alpha = jnp.exp2(ms[h] - m_new)
            new_l.append(alpha * ls[h] + jnp.sum(p, axis=0, keepdims=True))
            rows = slice(h * HEAD_DIM, (h + 1) * HEAD_DIM)
            acc_ref[rows, :] = alpha * acc_ref[rows, :] + _dot(vt_ref[rows, pl.ds(off, tk)], p.astype(BF16))
            new_m.append(m_new)
        return tuple(new_m), tuple(new_l)

    init = (tuple(jnp.full((1, tq), NEG / 2, F32) for _ in range(4)),
            tuple(jnp.zeros((1, tq), F32) for _ in range(4)))
    carry = lax.fori_loop(0, last, functools.partial(body, diag=False), init)
    _, ls = body(last, carry, True)
    out = jnp.concatenate([acc_ref[h * HEAD_DIM:(h + 1) * HEAD_DIM, :] / ls[h] for h in range(4)], axis=0)
    return out.T.astype(BF16)


def _tie_select(eq, rem, tri):
    parts = []
    for c in range(eq.shape[1] // LANES):
        e = eq[:, c * LANES:(c + 1) * LANES]
        pre = _dot(jnp.where(e, 1.0, 0.0).astype(BF16), tri)
        parts.append(jnp.where(e & (pre <= rem), 1, 0))
        rem = rem - pre[:, LANES - 1:LANES]
    return jnp.concatenate(parts, axis=1), rem


def _dsa_query_prep(iq, smq):
    lane = lax.broadcasted_iota(I32, iq.shape, 1)
    iqh = []
    for h in range(IDX_HEADS):
        r = iq if h == 0 else pltpu.roll(iq, LANES - IDX_DIM * h, 1)
        iqh.append(jnp.where(lane < IDX_DIM, r, 0.0).astype(BF16))
    wcol = [smq[:, SM_IW + h:SM_IW + h + 1] for h in range(IDX_HEADS)]
    return iqh, wcol


def _dsa_scores(iqh, wcol, ik_blk, contract32=False):
    sc = None
    for h in range(IDX_HEADS):
        a = iqh[h][:, :IDX_DIM] if contract32 else iqh[h]
        z = _dot_nt(a, ik_blk)
        t = jnp.maximum(z, 0.0) * wcol[h]
        sc = t if sc is None else sc + t
    return sc


def _dsa_prompt_kernel(qt_ref, iqt_ref, iwt_ref, k_ref, vt_ref, smk_ref, o_ref,
                       sc_ref, acc_ref, rem_ref, *, tq, tk, topk):
    j = pl.program_id(1)
    nkb = j + 1
    q0 = j * tq
    iqt = iqt_ref[...]
    pad = jnp.zeros((LANES - IDX_DIM, tq), BF16)
    iqh = [jnp.concatenate([iqt[IDX_DIM * h:IDX_DIM * (h + 1), :], pad], axis=0) for h in range(IDX_HEADS)]
    iw = iwt_ref[...]
    row = lax.broadcasted_iota(I32, (tk, tq), 0)
    col = lax.broadcasted_iota(I32, (tk, tq), 1)
    qchunk = (q0 + col) >> 6

    def visible(off):
        return ((off + row) >> 6) <= qchunk

    def score_body(kb, carry, diag):
        off = pl.multiple_of(kb * tk, tk)
        ik = smk_ref[pl.ds(off, tk), :].astype(BF16)
        sc = None
        for h in range(IDX_HEADS):
            t = jnp.maximum(_dot(ik, iqh[h]), 0.0) * iw[h:h + 1, :]
            sc = t if sc is None else sc + t
        sc_ref[pl.ds(off, tk), :] = jnp.where(visible(off), sc, -jnp.inf) if diag else sc
        return carry

    lax.fori_loop(0, j, functools.partial(score_body, diag=False), 0)
    score_body(j, 0, True)

    def count_cmp(trial, strict):
        def body(kb, c):
            off = pl.multiple_of(kb * tk, tk)
            s = sc_ref[pl.ds(off, tk), :]
            hit = (s > trial) if strict else (s >= trial)
            return c + jnp.sum(jnp.where(hit, 1, 0), axis=0, keepdims=True)
        return lax.fori_loop(0, nkb, body, jnp.zeros((1, tq), I32))

    thr = _kth_largest(lambda t: count_cmp(t, False), (1, tq), topk)
    rem_ref[0:1, :] = (topk - count_cmp(thr, True)).astype(F32)
    qh = _split_heads_qt(qt_ref[...])
    tri = _tri(tk)

    def sel_fn(off, diag):
        s = sc_ref[pl.ds(off, tk), :]
        eq = s == thr
        pre = _dot(tri, jnp.where(eq, 1.0, 0.0).astype(BF16))
        rem = rem_ref[0:1, :]
        keep = eq & (pre <= rem)
        rem_ref[0:1, :] = rem - pre[tk - 1:tk, :]
        sel = (s > thr) | keep
        return (visible(off) & sel) if diag else sel

    def logits_fn(off):
        k_blk = k_ref[pl.ds(off, tk), :]
        return [_dot(k_blk, qh[h]) for h in range(A_HEADS)]

    o_ref[...] = _flash_t(j, tk, tq, logits_fn, sel_fn, vt_ref, acc_ref)


def _dsa_prompt(aqt, iqt, iwt, akb, avt, sm, *, batch, seq, tq, topk, n_out):
    nq = seq // tq
    tk = tq
    qcol = lambda w: pl.BlockSpec((w, tq), lambda b, j: (0, b * nq + j))
    krow = lambda w: pl.BlockSpec((seq, w), lambda b, j: (b, 0))
    kern = functools.partial(_dsa_prompt_kernel, tq=tq, tk=tk, topk=topk)
    return pl.pallas_call(
        kern,
        grid=(batch, nq),
        in_specs=[qcol(HW), qcol(LANES), qcol(8), krow(HW),
                  pl.BlockSpec((HW, seq), lambda b, j: (0, b)), krow(LANES)],
        out_specs=pl.BlockSpec((tq, HW), lambda b, j: (b * nq + j, 0)),
        out_shape=jax.ShapeDtypeStruct((n_out, HW), BF16),
        scratch_shapes=[pltpu.VMEM((seq, tq), F32), pltpu.VMEM((HW, tq), F32), pltpu.VMEM((8, tq), F32)],
        compiler_params=_cparams(("parallel", "arbitrary")),
        name="dsa_prompt",
    )(aqt, iqt, iwt, akb, avt, sm)


def _dsa_sample_kernel(q_ref, iq_ref, smq_ref, kn_ref, vn_ref, ck_ref, cv_ref, cik_ref, prev_ref, o_ref,
                       *, tq, past, topk, nb):
    pad = LANES - tq
    lk = past + LANES
    lane = lax.broadcasted_iota(I32, (tq, LANES), 1)
    col = lax.broadcasted_iota(I32, (tq, lk), 1)
    vis = col < past + tq
    scs = []
    for s in range(nb):
        rows = slice(s * tq, (s + 1) * tq)
        smq = smq_ref[rows, :]
        iqh, wcol = _dsa_query_prep(iq_ref[rows, :], smq)
        sc_c = _dsa_scores(iqh, wcol, cik_ref[s].astype(BF16), contract32=True)
        ik_new = jnp.concatenate([smq.astype(BF16), jnp.zeros((pad, LANES), BF16)], axis=0)
        sc_n = _dsa_scores(iqh, wcol, ik_new)
        sc_n = jnp.where(lane < tq, sc_n, -jnp.inf)
        scs.append(jnp.concatenate([sc_c, sc_n], axis=1))

    def count_ge(trials):
        return tuple(jnp.sum(jnp.where(sc >= t, 1, 0), axis=1, keepdims=True) for sc, t in zip(scs, trials))

    thr_all = _kth_largest_multi(count_ge, (tq, 1), topk, nb)
    tri = _tri128()
    for s in range(nb):
        rows = slice(s * tq, (s + 1) * tq)
        sc = scs[s]
        thr = thr_all[s]
        n_gt = jnp.sum(jnp.where(sc > thr, 1, 0), axis=1, keepdims=True)
        need = (topk - n_gt).astype(F32)
        tie, _ = _tie_select(sc == thr, need, tri)
        sel = vis & ((sc > thr) | (tie > 0))
        k_all = jnp.concatenate([ck_ref[s].astype(BF16), kn_ref[rows, :], jnp.zeros((pad, HW), BF16)], axis=0)
        v_all = jnp.concatenate([cv_ref[s].astype(BF16), vn_ref[rows, :], jnp.zeros((pad, HW), BF16)], axis=0)
        qh = _split_heads_q(q_ref[rows, :])
        o_ref[rows, :] = _direct_attend([_dot_nt(qh[h], k_all) for h in range(A_HEADS)], sel, v_all)


def _dsa_sample(aq, iq, sm, akb, avb, cache_k, cache_v, cache_ik, o_prev, *, n_prompt, dbatch, dseq, topk):
    past = cache_k.shape[1]
    nb = 4 if dbatch % 4 == 0 else 1
    base = n_prompt // (nb * dseq)
    qrow = lambda w: pl.BlockSpec((nb * dseq, w), lambda b: (base + b, 0))
    crow = lambda w: pl.BlockSpec((nb, past, w), lambda b: (b, 0, 0))
    kern = functools.partial(_dsa_sample_kernel, tq=dseq, past=past, topk=topk, nb=nb)
    return pl.pallas_call(
        kern,
        grid=(dbatch // nb,),
        in_specs=[qrow(HW), qrow(LANES), qrow(LANES), qrow(HW), qrow(HW),
                  crow(HW), crow(HW), crow(IDX_DIM), pl.BlockSpec(memory_space=pl.ANY)],
        out_specs=qrow(HW),
        out_shape=jax.ShapeDtypeStruct(o_prev.shape, BF16),
        input_output_aliases={8: 0},
        compiler_params=_cparams(("parallel",)),
        name="dsa_sample",
    )(aq, iq, sm, akb, avb, cache_k, cache_v, cache_ik, o_prev)


def _split3(x):
    hi = x.astype(BF16)
    r1 = x - hi.astype(F32)
    mid = r1.astype(BF16)
    lo = (r1 - mid.astype(F32)).astype(BF16)
    return hi, mid, lo


def _fox_prompt_kernel(qt_ref, k_ref, vt_ref, smk_ref, o_ref, ck3_ref, acc_ref, *, tq, tk, seq):
    j = pl.program_id(1)
    tri = _tri(tk)

    @pl.when(j == 0)
    def _():
        lane = lax.broadcasted_iota(I32, (tk, LANES), 1)
        is_f = (lane >= SM_BF) & (lane < SM_BF + B_HEADS)
        carry = jnp.zeros((1, LANES), F32)
        for r in range(seq // tk):
            lf = jnp.where(is_f, smk_ref[r * tk:(r + 1) * tk, :], 0.0)
            cs = carry
            for piece in _split3(lf):
                cs = cs + _dot(tri, piece)
            carry = cs[tk - 1:tk, :]
            nb = cs * (-LOG2E)
            hi = nb.astype(BF16).astype(F32)
            r1 = nb - hi
            mid = r1.astype(BF16).astype(F32)
            lo = r1 - mid
            slab = jnp.where(is_f, hi, pltpu.roll(mid, B_HEADS, 1) + pltpu.roll(lo, 2 * B_HEADS, 1))
            ck3_ref[r * tk:(r + 1) * tk, :] = slab.astype(BF16)

    q0 = j * tq
    row = lax.broadcasted_iota(I32, (tk, tq), 0)
    col = lax.broadcasted_iota(I32, (tk, tq), 1)
    qpos = q0 + col
    qh = _split_heads_qt(qt_ref[...])
    srow = lax.broadcasted_iota(I32, (LANES, tq), 0)
    pick = [jnp.where((srow == SM_BF + h) | (srow == SM_BF + B_HEADS + h) | (srow == SM_BF + 2 * B_HEADS + h),
                      1.0, 0.0).astype(BF16) for h in range(B_HEADS)]
    qx = [jnp.concatenate([qh[h], pick[h]], axis=0) for h in range(B_HEADS)]

    def sel_fn(off, diag):
        return ((off + row) <= qpos) if diag else None

    def logits_fn(off):
        kx = jnp.concatenate([k_ref[pl.ds(off, tk), :], ck3_ref[pl.ds(off, tk), :]], axis=1)
        return [_dot(kx, qx[h]) for h in range(B_HEADS)]

    o_ref[...] = _flash_t(j, tk, tq, logits_fn, sel_fn, vt_ref, acc_ref)


def _fox_prompt(bqt, bkb, bvt, sm, *, batch, seq, tq, n_out):
    nq = seq // tq
    tk = tq
    krow = lambda w: pl.BlockSpec((seq, w), lambda b, j: (b, 0))
    kern = functools.partial(_fox_prompt_kernel, tq=tq, tk=tk, seq=seq)
    return pl.pallas_call(
        kern,
        grid=(batch, nq),
        in_specs=[pl.BlockSpec((HW, tq), lambda b, j: (0, b * nq + j)), krow(HW),
                  pl.BlockSpec((HW, seq), lambda b, j: (0, b)), krow(LANES)],
        out_specs=pl.BlockSpec((tq, HW), lambda b, j: (b * nq + j, 0)),
        out_shape=jax.ShapeDtypeStruct((n_out, HW), BF16),
        scratch_shapes=[pltpu.VMEM((seq, LANES), BF16), pltpu.VMEM((HW, tq), F32)],
        compiler_params=_cparams(("parallel", "arbitrary")),
        name="fox_prompt",
    )(bqt, bkb, bvt, sm)


def _fox_sample_kernel(q_ref, kn_ref, vn_ref, ck_ref, cv_ref, clf_ref, nlf_ref, prev_ref, o_ref, *, tq, past):
    pad = LANES - tq
    cum_c = _lane_cumsum(clf_ref[0])
    cum_n = _lane_cumsum(nlf_ref[0])
    bias_c = (cum_c[:, past - 1:past] - cum_c) * LOG2E
    bias_n = cum_n * (-LOG2E)
    bias = jnp.concatenate([bias_c, bias_n], axis=1)
    lk = past + LANES
    row = lax.broadcasted_iota(I32, (tq, lk), 0)
    col = lax.broadcasted_iota(I32, (tq, lk), 1)
    sel = col <= past + row
    k_all = jnp.concatenate([ck_ref[0].astype(BF16), kn_ref[...], jnp.zeros((pad, HW), BF16)], axis=0)
    v_all = jnp.concatenate([cv_ref[0].astype(BF16), vn_ref[...], jnp.zeros((pad, HW), BF16)], axis=0)
    qh = _split_heads_q(q_ref[...])
    logits = [_dot_nt(qh[h], k_all) + bias[h:h + 1, :] for h in range(B_HEADS)]
    o_ref[...] = _direct_attend(logits, sel, v_all)


def _fox_sample(bq, bkb, bvb, cache_k, cache_v, cache_lft, new_lft, o_prev, *, n_prompt, dbatch, dseq):
    past = cache_k.shape[1]
    base = n_prompt // dseq
    qrow = lambda w: pl.BlockSpec((dseq, w), lambda b: (base + b, 0))
    crow = lambda w: pl.BlockSpec((1, past, w), lambda b: (b, 0, 0))
    kern = functools.partial(_fox_sample_kernel, tq=dseq, past=past)
    return pl.pallas_call(
        kern,
        grid=(dbatch,),
        in_specs=[qrow(HW), qrow(HW), qrow(HW), crow(HW), crow(HW),
                  pl.BlockSpec((1, 8, past), lambda b: (b, 0, 0)),
                  pl.BlockSpec((1, 8, LANES), lambda b: (b, 0, 0)),
                  pl.BlockSpec(memory_space=pl.ANY)],
        out_specs=qrow(HW),
        out_shape=jax.ShapeDtypeStruct(o_prev.shape, BF16),
        input_output_aliases={7: 0},
        compiler_params=_cparams(("parallel",)),
        name="fox_sample",
    )(bq, bkb, bvb, cache_k, cache_v, cache_lft, new_lft, o_prev)


def _mla_logits(q, kfull_blk):
    return [_dot_nt(q[:, h * LANES:(h + 1) * LANES], kfull_blk[:, h * LANES:(h + 1) * LANES])
            for h in range(C_HEADS)]


def _mla_prompt_kernel(qt_ref, ckv_ref, ckvt_ref, smk_ref, w1_ref, w2_ref, w1vt_ref, o_ref,
                       kf_ref, vt_ref, acc_ref, *, tq, tk, seq):
    j = pl.program_id(1)

    @pl.when(j == 0)
    def _():
        rc = 512 if seq % 512 == 0 else tk
        for r in range(seq // rc):
            kf = (_dot(ckv_ref[r * rc:(r + 1) * rc, :].astype(BF16), w1_ref[:, 0:512])
                  + _dot(smk_ref[r * rc:(r + 1) * rc, :].astype(BF16), w2_ref[:, 0:512]))
            kf_ref[r * rc:(r + 1) * rc, :] = kf.astype(BF16)
            vt_ref[:, r * rc:(r + 1) * rc] = _dot(w1vt_ref[...], ckvt_ref[:, r * rc:(r + 1) * rc]).astype(BF16)

    q0 = j * tq
    row = lax.broadcasted_iota(I32, (tk, tq), 0)
    col = lax.broadcasted_iota(I32, (tk, tq), 1)
    qchunk = (q0 + col) >> 6
    qt = qt_ref[...]

    def sel_fn(off, diag):
        return (((off + row) >> 6) <= qchunk) if diag else None

    def logits_fn(off):
        kf = kf_ref[pl.ds(off, tk), :]
        return [_dot(kf[:, h * LANES:(h + 1) * LANES], qt[h * LANES:(h + 1) * LANES, :])
                for h in range(C_HEADS)]

    o_ref[...] = _flash_t(j, tk, tq, logits_fn, sel_fn, vt_ref, acc_ref)


def _mla_prompt(qct, ckv, ckvt, sm, w1, w2, w1vt, *, batch, seq, tq, n_out):
    nq = seq // tq
    tk = tq
    krow = lambda w: pl.BlockSpec((seq, w), lambda b, j: (b, 0))
    full = lambda a: pl.BlockSpec(a.shape, lambda b, j: (0,) * a.ndim)
    kern = functools.partial(_mla_prompt_kernel, tq=tq, tk=tk, seq=seq)
    return pl.pallas_call(
        kern,
        grid=(batch, nq),
        in_specs=[pl.BlockSpec((512, tq), lambda b, j: (0, b * nq + j)), krow(LANES),
                  pl.BlockSpec((LANES, seq), lambda b, j: (0, b)), krow(LANES),
                  full(w1), full(w2), full(w1vt)],
        out_specs=pl.BlockSpec((tq, HW), lambda b, j: (b * nq + j, 0)),
        out_shape=jax.ShapeDtypeStruct((n_out, HW), BF16),
        scratch_shapes=[pltpu.VMEM((seq, 512), BF16), pltpu.VMEM((HW, seq), BF16),
                        pltpu.VMEM((HW, tq), F32)],
        compiler_params=_cparams(("parallel", "arbitrary")),
        name="mla_prompt",
    )(qct, ckv, ckvt, sm, w1, w2, w1vt)


def _mla_sample_kernel(q_ref, ckvn_ref, smn_ref, cckv_ref, ckr_ref, w1_ref, w2_ref, w2c_ref, prev_ref, o_ref,
                       *, tq, past):
    pad = LANES - tq
    kv_c = _dot(cckv_ref[0].astype(BF16), w1_ref[...]) + _dot(ckr_ref[0].astype(BF16), w2c_ref[...])
    kv_n = _dot(ckvn_ref[...].astype(BF16), w1_ref[...]) + _dot(smn_ref[...].astype(BF16), w2_ref[...])
    kv = jnp.concatenate([kv_c, kv_n, jnp.zeros((pad, 768), F32)], axis=0).astype(BF16)
    lk = past + LANES
    col = lax.broadcasted_iota(I32, (tq, lk), 1)
    sel = col < past + tq
    o_ref[...] = _direct_attend(_mla_logits(q_ref[...], kv[:, :512]), sel, kv[:, 512:])


def _mla_sample(qc, ckv, sm, cache_ckv, cache_kr, w1, w2, w2c, o_prev, *, n_prompt, dbatch, dseq):
    past = cache_ckv.shape[1]
    base = n_prompt // dseq
    qrow = lambda w: pl.BlockSpec((dseq, w), lambda b: (base + b, 0))
    crow = lambda w: pl.BlockSpec((1, past, w), lambda b: (b, 0, 0))
    full = lambda a: pl.BlockSpec(a.shape, lambda b: (0,) * a.ndim)
    kern = functools.partial(_mla_sample_kernel, tq=dseq, past=past)
    return pl.pallas_call(
        kern,
        grid=(dbatch,),
        in_specs=[qrow(512), qrow(LANES), qrow(LANES), crow(C_KV_RANK), crow(C_ROPE),
                  full(w1), full(w2), full(w2c), pl.BlockSpec(memory_space=pl.ANY)],
        out_specs=qrow(HW),
        out_shape=jax.ShapeDtypeStruct(o_prev.shape, BF16),
        input_output_aliases={8: 0},
        compiler_params=_cparams(("parallel",)),
        name="mla_sample",
    )(qc, ckv, sm, cache_ckv, cache_kr, w1, w2, w2c, o_prev)


def _merge_kernel(x_ref, g1_ref, oa_ref, ob_ref, oc_ref, wg_ref, wua_ref, wub_ref, wuc_ref, wout_ref,
                  xo_ref):
    x = x_ref[...]
    d = x.shape[1]
    h = _rmsnorm(x, g1_ref[...]).astype(BF16)
    gates = jax.nn.sigmoid(_dot(h, wg_ref[...]))
    merged = (gates[:, 0:d] * _dot(oa_ref[...], wua_ref[...])
              + gates[:, d:2 * d] * _dot(ob_ref[...], wub_ref[...])
              + gates[:, 2 * d:3 * d] * _dot(oc_ref[...], wuc_ref[...]))
    xo_ref[...] = x + _dot(merged.astype(BF16), wout_ref[...])


def _merge(x, g1, oa, ob, oc, wg, wua, wub, wuc, wout, *, tm):
    n, d = x.shape
    row = lambda w: pl.BlockSpec((tm, w), lambda i: (i, 0))
    full = lambda a: pl.BlockSpec(a.shape, lambda i: (0,) * a.ndim)
    return pl.pallas_call(
        _merge_kernel,
        grid=(n // tm,),
        in_specs=[row(d), full(g1), row(HW), row(HW), row(HW),
                  full(wg), full(wua), full(wub), full(wuc), full(wout)],
        out_specs=row(d),
        out_shape=jax.ShapeDtypeStruct((n, d), F32),
        compiler_params=_cparams(("parallel",)),
        name="merge",
    )(x, g1, oa, ob, oc, wg, wua, wub, wuc, wout)


WORD = I32


def _row_words(x):
    u = pltpu.bitcast(x.astype(BF16).astype(F32), I32)
    return u | lax.shift_right_logical(u, 16)


def _bcast_rows_bf16(words_row, rows):
    return pltpu.bitcast(jnp.broadcast_to(words_row, (rows // 2, words_row.shape[1])), BF16)


def _top_values(s, k, with_rank=False):
    vals = []
    rank = jnp.full(s.shape, float(k), F32)
    for r in range(k):
        m = jnp.max(s, axis=0, keepdims=True)
        vals.append(m)
        hit = s == m
        if with_rank:
            rank = jnp.where(hit, float(r), rank)
        s = jnp.where(hit, -jnp.inf, s)
    return (vals, rank) if with_rank else vals


def _peer_kernel(x_ref, g2_ref, wqt_ref, k1_ref, k2_ref, u_ref, vt_ref, o_ref,
                 ht_ref, qt_ref, acc_ref, rank_ref, e2_ref, cnt_ref, c_ref, v2_ref, cand_ref,
                 *, nk, na, grp):
    k = pl.program_id(1)
    tm = x_ref.shape[0]
    half = PEER_QDIM // 2

    @pl.when(k == 0)
    def _():
        h = _rmsnorm(x_ref[...], g2_ref[...])
        ht = h.T.astype(BF16)
        ht_ref[...] = ht
        qt_ref[...] = _dot(wqt_ref[...], ht).astype(BF16)
        acc_ref[...] = jnp.zeros(acc_ref.shape, F32)

        def select_body(hh, carry):
            qrow = pl.multiple_of(hh * PEER_QDIM, PEER_QDIM)
            krow = pl.multiple_of(hh * nk, nk)
            q1 = qt_ref[pl.ds(qrow, half), :]
            q2 = qt_ref[pl.ds(qrow + half, half), :]
            s1 = _dot(k1_ref[pl.ds(krow, nk), :], q1)
            s2 = _dot(k2_ref[pl.ds(krow, nk), :], q2)
            v1 = _top_values(s1, PEER_TOPK)
            v2, rank2 = _top_values(s2, PEER_TOPK, with_rank=True)
            for jj in range(PEER_TOPK):
                v2_ref[jj:jj + 1, :] = v2[jj]
            cand_ref[...] = jnp.full(cand_ref.shape, -jnp.inf, F32)
            off = 0
            for i in range(PEER_TOPK):
                nj = PEER_TOPK // (i + 1)
                cand_ref[off:off + nj, :] = v1[i] + v2_ref[0:nj, :]
                off += nj
            cand = cand_ref[...]
            tau = _top_values(cand, PEER_TOPK)[-1]
            top = v1[0] + v2[0]
            z = jnp.sum(jnp.where(cand >= tau, jnp.exp(cand - top), 0.0), axis=0, keepdims=True)
            cnt = jnp.zeros((nk, tm), F32)
            for jj in range(PEER_TOPK):
                cnt = jnp.where((s1 + v2[jj]) >= tau, float(jj + 1), cnt)
            rank_ref[hh] = rank2.astype(BF16)
            e2_ref[hh] = jnp.exp(s2 - v2[0]).astype(BF16)
            cnt_ref[hh] = _row_words(cnt)
            c_ref[hh] = _row_words(jnp.exp(s1 - v1[0]) / z)
            return carry

        lax.fori_loop(0, PEER_HEADS, select_body, 0)

    a0 = pl.multiple_of(k * na, na)
    cnt_rows = [cnt_ref[hh, pl.ds(a0, na), :] for hh in range(PEER_HEADS)]
    c_rows = [c_ref[hh, pl.ds(a0, na), :] for hh in range(PEER_HEADS)]
    zero = jnp.zeros((nk, tm), BF16)
    total = None
    for p in range(na // grp):
        rows = slice(p * grp * nk, (p + 1) * grp * nk)
        act = jax.nn.gelu(_dot(u_ref[rows, :], ht_ref[...]).astype(BF16))
        gated = []
        for ai in range(p * grp, (p + 1) * grp):
            w = None
            for hh in range(PEER_HEADS):
                cb = _bcast_rows_bf16(cnt_rows[hh][ai:ai + 1, :], nk)
                cc = _bcast_rows_bf16(c_rows[hh][ai:ai + 1, :], nk)
                t = jnp.where(rank_ref[hh] < cb, e2_ref[hh], zero) * cc
                w = t if w is None else w + t
            r0 = (ai - p * grp) * nk
            gated.append(w * act[r0:r0 + nk, :])
        part = _dot(vt_ref[:, rows], jnp.concatenate(gated, axis=0))
        total = part if total is None else total + part
    acc_ref[...] += total

    @pl.when(k == pl.num_programs(1) - 1)
    def _():
        o_ref[...] = x_ref[...] + acc_ref[...].T


def _peer(x, g2, wqt, k1, k2, u, vt, *, tm, na):
    n, d = x.shape
    nk = k1.shape[0] // PEER_HEADS
    te = na * nk
    nchunks = nk // na
    ncand = sum(PEER_TOPK // (i + 1) for i in range(PEER_TOPK))
    ncand = -(-ncand // 8) * 8
    full = lambda a: pl.BlockSpec(a.shape, lambda i, k: (0,) * a.ndim)
    kern = functools.partial(_peer_kernel, nk=nk, na=na, grp=na)
    return pl.pallas_call(
        kern,
        grid=(n // tm, nchunks),
        in_specs=[pl.BlockSpec((tm, d), lambda i, k: (i, 0)), full(g2), full(wqt), full(k1), full(k2),
                  pl.BlockSpec((te, d), lambda i, k: (k, 0)),
                  pl.BlockSpec((None, d, te), lambda i, k: (k, 0, 0))],
        out_specs=pl.BlockSpec((tm, d), lambda i, k: (i, 0)),
        out_shape=jax.ShapeDtypeStruct((n, d), F32),
        scratch_shapes=[pltpu.VMEM((d, tm), BF16), pltpu.VMEM((PEER_HEADS * PEER_QDIM, tm), BF16),
                        pltpu.VMEM((d, tm), F32),
                        pltpu.VMEM((PEER_HEADS, nk, tm), BF16), pltpu.VMEM((PEER_HEADS, nk, tm), BF16),
                        pltpu.VMEM((PEER_HEADS, nk, tm), WORD), pltpu.VMEM((PEER_HEADS, nk, tm), WORD),
                        pltpu.VMEM((PEER_TOPK, tm), F32), pltpu.VMEM((ncand, tm), F32)],
        compiler_params=_cparams(("parallel", "arbitrary")),
        name="peer",
    )(x, g2, wqt, k1, k2, u, vt)


def _final_norm_kernel(x_ref, g_ref, op_ref, os_ref, *, npt):
    y = _rmsnorm(x_ref[...], g_ref[...])
    i = pl.program_id(0)

    @pl.when(i < npt)
    def _():
        op_ref[...] = y

    @pl.when(i >= npt)
    def _():
        os_ref[...] = y


def _final_norm(x, g, *, tm, n_prompt):
    n, d = x.shape
    npt = n_prompt // tm
    return pl.pallas_call(
        functools.partial(_final_norm_kernel, npt=npt),
        grid=(n // tm,),
        in_specs=[pl.BlockSpec((tm, d), lambda i: (i, 0)), pl.BlockSpec((1, d), lambda i: (0, 0))],
        out_specs=[pl.BlockSpec((tm, d), lambda i: (jnp.minimum(i, npt - 1), 0)),
                   pl.BlockSpec((tm, d), lambda i: (jnp.maximum(i - npt, 0), 0))],
        out_shape=[jax.ShapeDtypeStruct((n_prompt, d), F32), jax.ShapeDtypeStruct((n - n_prompt, d), F32)],
        compiler_params=_cparams(("arbitrary",)),
        name="final_norm",
    )(x, g)


def _rope_tables(pos):
    rows = pos.shape[0]

    def cs(half):
        inv = jnp.power(ROPE_THETA, -jnp.arange(half, dtype=F32) / half)
        ang = pos.astype(F32)[:, None] * inv[None, :]
        c, s = jnp.cos(ang), jnp.sin(ang)
        return jnp.concatenate([c, c], axis=1), jnp.concatenate([-s, s], axis=1)

    c64, s64 = cs(HEAD_DIM // 2)
    c32, s32 = cs(IDX_DIM // 2)
    one32 = jnp.ones((rows, 32), F32)
    zero32 = jnp.zeros((rows, 32), F32)
    c_sm = jnp.concatenate([c32, c32, one32, one32], axis=1)
    s_sm = jnp.concatenate([s32, s32, zero32, zero32], axis=1)
    c_qc = jnp.concatenate([one32, one32, c32, one32], axis=1)
    s_qc = jnp.concatenate([zero32, zero32, s32, zero32], axis=1)
    ctab = jnp.concatenate([jnp.tile(c64, (1, 4)), jnp.tile(c32, (1, 4)), c_sm, jnp.tile(c_qc, (1, 4))], axis=1)
    stab = jnp.concatenate([jnp.tile(s64, (1, 4)), jnp.tile(s32, (1, 4)), s_sm, jnp.tile(s_qc, (1, 4))], axis=1)
    return ctab, stab


def _split_w_in(w_in):
    sizes = (HW, HW, HW, IDX_HEADS * IDX_DIM, IDX_DIM, IDX_HEADS, HW, HW, HW, B_HEADS,
             C_Q_RANK, C_KV_RANK, C_ROPE)
    offs = np.cumsum((0,) + sizes)
    parts = [w_in[..., int(offs[i]):int(offs[i + 1])] for i in range(len(sizes))]
    gates = w_in[..., int(offs[-1]):]
    return parts, gates


def _pack_weights(w_in, w_uq, w_ukv, b_forget):
    depth, d, _ = w_in.shape
    (a_q, a_k, a_v, i_q, i_k, i_w, b_q, b_k, b_v, b_f, c_q, c_kv, c_kr), gates = _split_w_in(w_in)
    z = lambda w: jnp.zeros((depth, d, w), F32)
    small = jnp.concatenate([i_k, c_kr, b_f, z(4), i_w, z(LANES - SM_IW - IDX_HEADS)], axis=-1)
    wmain = jnp.concatenate([a_q, a_k, a_v, b_q, b_k, b_v, c_q, c_kv, i_q, small], axis=-1).astype(BF16)
    wg = gates.astype(BF16)

    uq = w_uq.reshape(depth, C_Q_RANK, C_HEADS, C_NOPE + C_ROPE)
    uq = jnp.concatenate([uq, jnp.zeros((depth, C_Q_RANK, C_HEADS, 32), F32)], axis=-1)
    wuq = uq.reshape(depth, C_Q_RANK, C_HEADS * LANES).astype(BF16)

    ukv = w_ukv.reshape(depth, C_KV_RANK, C_HEADS, C_NOPE + C_VDIM)
    kn = jnp.concatenate([ukv[..., :C_NOPE], jnp.zeros((depth, C_KV_RANK, C_HEADS, 64), F32)], axis=-1)
    w1 = jnp.concatenate([kn.reshape(depth, C_KV_RANK, 512),
                          ukv[..., C_NOPE:].reshape(depth, C_KV_RANK, HW)], axis=-1).astype(BF16)
    w1vt = jnp.swapaxes(w1[:, :, 512:], 1, 2)
    eye = np.zeros((C_ROPE, 768), np.float32)
    for h in range(C_HEADS):
        for r in range(C_ROPE):
            eye[r, h * LANES + C_NOPE + r] = 1.0
    w2c = jnp.asarray(eye).astype(BF16)
    w2_np = np.zeros((LANES, 768), np.float32)
    w2_np[SM_KR:SM_KR + C_ROPE] = eye
    w2 = jnp.asarray(w2_np).astype(BF16)

    aux = jnp.zeros((depth, 8, LANES), F32)
    aux = aux.at[:, 0, SM_BF:SM_BF + B_HEADS].set(b_forget)
    scale = np.ones((LANES,), np.float32)
    scale[SM_IW:SM_IW + IDX_HEADS] = (IDX_HEADS * IDX_DIM) ** -0.5
    aux = aux.at[:, 1, :].set(jnp.asarray(scale))
    return wmain, wg, wuq, w1, w1vt, w2, w2c, aux


def _tile_rows(n):
    for t in (512, 256, 128):
        if n % t == 0:
            return t
    raise ValueError(f"token count {n} is not a multiple of 128")


def kernel(x_prompt, x_sample, cache_k_a, cache_v_a, cache_kidx_a, cache_k_b, cache_v_b, cache_logf_b,
           cache_ckv_c, cache_krope_c, norm1_g, w_in, c_q_norm_g, c_kv_norm_g, w_uq, w_ukv, b_forget,
           w_up_a, w_up_b, w_up_c, w_out, norm2_g, peer_w_q, peer_keys, peer_u, peer_v, final_norm_g):
    batch, seq, d = x_prompt.shape
    dbatch, dseq, _ = x_sample.shape
    depth = w_in.shape[0]
    past = cache_k_a.shape[2]
    n_p = batch * seq
    n_s = dbatch * dseq
    n = n_p + n_s
    nk = peer_keys.shape[3]
    assert dseq == CHUNK and past % LANES == 0 and seq % 256 == 0

    tm = min(_tile_rows(seq), _tile_rows(n_s))
    tq = 512 if seq % 512 == 0 else 256
    topk_p = min(TOPK_MAX, seq // 4)
    topk_s = min(TOPK_MAX, (past + dseq) // 4)

    pos = jnp.concatenate([jnp.arange(seq, dtype=jnp.int32),
                           past + (jnp.arange(tm, dtype=jnp.int32) % dseq)])
    ctab, stab = _rope_tables(pos)

    wmain, wg, wuq, w1, w1vt, w2, w2c, aux = _pack_weights(w_in, w_uq, w_ukv, b_forget)
    wua, wub, wuc, wout = (w.astype(BF16) for w in (w_up_a, w_up_b, w_up_c, w_out))
    wqt = jnp.swapaxes(peer_w_q, 1, 2).astype(BF16)
    half = PEER_QDIM // 2
    k1 = peer_keys[:, :, 0].reshape(depth, PEER_HEADS * nk, half).astype(BF16)
    k2 = peer_keys[:, :, 1].reshape(depth, PEER_HEADS * nk, half).astype(BF16)
    pu = peer_u.astype(BF16)
    na = 8 if nk % 8 == 0 else nk
    pvt = peer_v.astype(BF16).reshape(depth, nk // na, na * nk, d).transpose(0, 1, 3, 2)
    clf_t = jnp.swapaxes(cache_logf_b, 2, 3)
    clf_t = jnp.concatenate([clf_t, jnp.zeros_like(clf_t)], axis=2)

    g_row = lambda g: g.reshape(1, -1)
    x = jnp.concatenate([x_prompt.reshape(n_p, d), x_sample.reshape(n_s, d)], axis=0)
    nr = len(ROW_WIDTHS)
    rows_p = [[] for _ in range(nr)]
    rows_s = [[] for _ in range(nr)]

    for l in range(depth):
        g1 = g_row(norm1_g[l])
        outs = _inproj(x, g1, wmain[l], wuq[l], g_row(c_q_norm_g[l]), g_row(c_kv_norm_g[l]), aux[l],
                       ctab, stab, tm=tm, n_prompt=n_p, seq=seq)
        (aq, akb, avb, bq, bkb, bvb, qc, ckv, iq, sm, lft,
         aqt, avt, bqt, bvt, qct, iqt, ckvt, iwt) = outs[:19]
        for i in range(nr):
            rows_p[i].append(outs[19 + i])
            rows_s[i].append(outs[19 + nr + i])

        new_lft = lft[:, n_p:].reshape(8, dbatch, dseq).transpose(1, 0, 2)
        new_lft = jnp.concatenate([new_lft, jnp.zeros((dbatch, 8, LANES - dseq), F32)], axis=2)

        oa = _dsa_prompt(aqt, iqt, iwt, akb, avt, sm, batch=batch, seq=seq, tq=tq, topk=topk_p, n_out=n)
        oa = _dsa_sample(aq, iq, sm, akb, avb,
                         cache_k_a[l].reshape(dbatch, past, HW), cache_v_a[l].reshape(dbatch, past, HW),
                         cache_kidx_a[l], oa, n_prompt=n_p, dbatch=dbatch, dseq=dseq, topk=topk_s)
        ob = _fox_prompt(bqt, bkb, bvt, sm, batch=batch, seq=seq, tq=tq, n_out=n)
        ob = _fox_sample(bq, bkb, bvb,
                         cache_k_b[l].reshape(dbatch, past, HW), cache_v_b[l].reshape(dbatch, past, HW),
                         clf_t[l], new_lft, ob, n_prompt=n_p, dbatch=dbatch, dseq=dseq)
        oc = _mla_prompt(qct, ckv, ckvt, sm, w1[l], w2, w1vt[l], batch=batch, seq=seq, tq=tq, n_out=n)
        oc = _mla_sample(qc, ckv, sm, cache_ckv_c[l], cache_krope_c[l], w1[l], w2, w2c, oc,
                         n_prompt=n_p, dbatch=dbatch, dseq=dseq)

        x = _merge(x, g1, oa, ob, oc, wg[l], wua[l], wub[l], wuc[l], wout[l], tm=tm)
        x = _peer(x, g_row(norm2_g[l]), wqt[l], k1[l], k2[l], pu[l], pvt[l], tm=tm, na=na)

    y_p, y_s = _final_norm(x, g_row(final_norm_g), tm=tm, n_prompt=n_p)
    y_p = y_p.reshape(batch, seq, d)
    y_s = y_s.reshape(dbatch, dseq, d)

    tails = [(A_HEADS, HEAD_DIM), (A_HEADS, HEAD_DIM), (IDX_DIM,), (B_HEADS, HEAD_DIM), (B_HEADS, HEAD_DIM),
             (B_HEADS,), (C_KV_RANK,), (C_ROPE,)]
    outs_p = [jnp.stack(r).reshape((depth, batch, seq) + t) for r, t in zip(rows_p, tails)]
    outs_s = [jnp.stack(r).reshape((depth, dbatch, dseq) + t) for r, t in zip(rows_s, tails)]
    return (y_p, y_s, *outs_p, *outs_s)
```

```python
import functools

import jax
import jax.numpy as jnp
import numpy as np
from jax import lax
from jax.experimental import pallas as pl
from jax.experimental.pallas import tpu as pltpu

F32 = jnp.float32
BF16 = jnp.bfloat16
I32 = jnp.int32

CHUNK = 64
HEAD_DIM = 64
ROPE_THETA = 10000.0
NORM_EPS = 1e-6
A_HEADS = 4
IDX_HEADS = 4
IDX_DIM = 32
TOPK_MAX = 256
B_HEADS = 4
C_HEADS = 4
C_NOPE = 64
C_ROPE = 32
C_VDIM = 64
C_Q_RANK = 256
C_KV_RANK = 128
N_BRANCHES = 3
PEER_HEADS = 8
PEER_QDIM = 128
PEER_TOPK = 16

HW = A_HEADS * HEAD_DIM
LANES = 128
VMEM_LIMIT = 56 * 1024 * 1024

SM_IK = 0
SM_KR = 32
SM_BF = 64
SM_IW = 72

MC_AQ, MC_AK, MC_AV = 0, 256, 512
MC_BQ, MC_BK, MC_BV = 768, 1024, 1280
MC_CQ = 1536
MC_CKV = 1792
MC_IQ = 1920
MC_SM = 2048
MAIN_COLS = 2176

RT_64 = 0
RT_IQ = 256
RT_SM = 384
RT_QC = 512
RT_COLS = 1024

LOG2E = 1.4426950408889634
NEG = -1e30
KEY_NEG_INF = -2139095041
INT_MIN = -2147483648


def _cparams(sem):
    return pltpu.CompilerParams(dimension_semantics=sem, vmem_limit_bytes=VMEM_LIMIT)


def _rmsnorm(x, g):
    ms = jnp.mean(x * x, axis=-1, keepdims=True)
    return x * lax.rsqrt(ms + NORM_EPS) * g


def _dot(a, b):
    return jnp.dot(a, b, preferred_element_type=F32)


def _dot_nt(a, b):
    return lax.dot_general(a, b, (((1,), (1,)), ((), ())), preferred_element_type=F32)


def _rope(v, c, s, half):
    w = v.shape[-1]
    lane = lax.broadcasted_iota(I32, v.shape, 1)
    first = (lane & (2 * half - 1)) < half
    vr = jnp.where(first, pltpu.roll(v, w - half, 1), pltpu.roll(v, half, 1))
    return v * c + vr * s


def _inproj_kernel(x_ref, g1_ref, wmain_ref, wuq_ref, gq_ref, gkv_ref, aux_ref, ct_ref, st_ref,
                   aq_ref, akb_ref, avb_ref, bq_ref, bkb_ref, bvb_ref,
                   qc_ref, ckv_ref, iq_ref, sm_ref, lft_ref,
                   aqt_ref, avt_ref, bqt_ref, bvt_ref, qct_ref, iqt_ref, ckvt_ref, iwt_ref,
                   *row_refs, npt):
    x = x_ref[...]
    h = _rmsnorm(x, g1_ref[...]).astype(BF16)
    y = _dot(h, wmain_ref[...])
    ct = ct_ref[...]
    st = st_ref[...]

    c64 = ct[:, RT_64:RT_64 + HW]
    s64 = st[:, RT_64:RT_64 + HW]
    aq = _rope(y[:, MC_AQ:MC_AQ + HW], c64, s64, HEAD_DIM // 2)
    aq = aq * (HEAD_DIM ** -0.5 * LOG2E)
    aq_ref[...] = aq.astype(BF16)
    aqt_ref[...] = aq.T.astype(BF16)
    ak = _rope(y[:, MC_AK:MC_AK + HW], c64, s64, HEAD_DIM // 2)
    akb_ref[...] = ak.astype(BF16)
    av = y[:, MC_AV:MC_AV + HW]
    avb_ref[...] = av.astype(BF16)
    avt_ref[...] = av.T.astype(BF16)

    bq = y[:, MC_BQ:MC_BQ + HW] * (HEAD_DIM ** -0.5 * LOG2E)
    bq_ref[...] = bq.astype(BF16)
    bqt_ref[...] = bq.T.astype(BF16)
    bk = y[:, MC_BK:MC_BK + HW]
    bkb_ref[...] = bk.astype(BF16)
    bv = y[:, MC_BV:MC_BV + HW]
    bvb_ref[...] = bv.astype(BF16)
    bvt_ref[...] = bv.T.astype(BF16)

    cq = _rmsnorm(y[:, MC_CQ:MC_CQ + C_Q_RANK], gq_ref[...]).astype(BF16)
    qc = _dot(cq, wuq_ref[...])
    qc = _rope(qc, ct[:, RT_QC:RT_QC + 512], st[:, RT_QC:RT_QC + 512], C_ROPE // 2)
    qc = qc * ((C_NOPE + C_ROPE) ** -0.5 * LOG2E)
    qc_ref[...] = qc.astype(BF16)
    qct_ref[...] = qc.T.astype(BF16)

    ckv = _rmsnorm(y[:, MC_CKV:MC_CKV + C_KV_RANK], gkv_ref[...])
    ckv_ref[...] = ckv
    ckvt_ref[...] = ckv.T.astype(BF16)

    iq = _rope(y[:, MC_IQ:MC_IQ + LANES], ct[:, RT_IQ:RT_IQ + LANES],
               st[:, RT_IQ:RT_IQ + LANES], IDX_DIM // 2)
    iq_ref[...] = iq
    iqt_ref[...] = iq.T.astype(BF16)

    sm = _rope(y[:, MC_SM:MC_SM + LANES], ct[:, RT_SM:RT_SM + LANES],
               st[:, RT_SM:RT_SM + LANES], IDX_DIM // 2)
    lane = lax.broadcasted_iota(I32, sm.shape, 1)
    is_f = (lane >= SM_BF) & (lane < SM_BF + B_HEADS)
    z = sm + aux_ref[0:1, :]
    logsig = jnp.minimum(z, 0.0) - jnp.log1p(jnp.exp(-jnp.abs(z)))
    sm = jnp.where(is_f, logsig, sm * aux_ref[1:2, :])
    sm_ref[...] = sm
    smt = sm.T
    lft_ref[...] = smt[SM_BF:SM_BF + 8, :]
    iwt_ref[...] = smt[SM_IW:SM_IW + 8, :]

    new_rows = (ak, av, sm[:, SM_IK:SM_IK + IDX_DIM], bk, bv, sm[:, SM_BF:SM_BF + B_HEADS],
                ckv, sm[:, SM_KR:SM_KR + C_ROPE])
    nr = len(new_rows)
    i = pl.program_id(0)

    @pl.when(i < npt)
    def _():
        for r, v in zip(row_refs[:nr], new_rows):
            r[...] = v

    @pl.when(i >= npt)
    def _():
        for r, v in zip(row_refs[nr:], new_rows):
            r[...] = v


ROW_WIDTHS = (HW, HW, IDX_DIM, HW, HW, B_HEADS, C_KV_RANK, C_ROPE)


def _inproj(x, g1, wmain, wuq, gq, gkv, aux, ctab, stab, *, tm, n_prompt, seq):
    n, d = x.shape
    nt = n // tm
    npt = n_prompt // tm
    tps = seq // tm

    def tmap(i):
        return (jnp.where(i < npt, i % tps, tps), 0)

    row = lambda w: pl.BlockSpec((tm, w), lambda i: (i, 0))
    prow = lambda w: pl.BlockSpec((tm, w), lambda i: (jnp.minimum(i, npt - 1), 0))
    srow = lambda w: pl.BlockSpec((tm, w), lambda i: (jnp.maximum(i - npt, 0), 0))
    full = lambda a: pl.BlockSpec(a.shape, lambda i: (0,) * a.ndim)
    out_shapes = [
        jax.ShapeDtypeStruct((n, HW), BF16),
        jax.ShapeDtypeStruct((n, HW), BF16),
        jax.ShapeDtypeStruct((n, HW), BF16),
        jax.ShapeDtypeStruct((n, HW), BF16),
        jax.ShapeDtypeStruct((n, HW), BF16),
        jax.ShapeDtypeStruct((n, HW), BF16),
        jax.ShapeDtypeStruct((n, 512), BF16),
        jax.ShapeDtypeStruct((n, LANES), F32),
        jax.ShapeDtypeStruct((n, LANES), F32),
        jax.ShapeDtypeStruct((n, LANES), F32),
        jax.ShapeDtypeStruct((8, n), F32),
        jax.ShapeDtypeStruct((HW, n), BF16),
        jax.ShapeDtypeStruct((HW, n), BF16),
        jax.ShapeDtypeStruct((HW, n), BF16),
        jax.ShapeDtypeStruct((HW, n), BF16),
        jax.ShapeDtypeStruct((512, n), BF16),
        jax.ShapeDtypeStruct((LANES, n), BF16),
        jax.ShapeDtypeStruct((LANES, n), BF16),
        jax.ShapeDtypeStruct((8, n), F32),
    ]
    out_shapes += [jax.ShapeDtypeStruct((n_prompt, w), F32) for w in ROW_WIDTHS]
    out_shapes += [jax.ShapeDtypeStruct((n - n_prompt, w), F32) for w in ROW_WIDTHS]
    col = lambda w: pl.BlockSpec((w, tm), lambda i: (0, i))
    out_specs = [row(HW)] * 6 + [row(512), row(LANES), row(LANES), row(LANES), col(8),
                                 col(HW), col(HW), col(HW), col(HW), col(512), col(LANES), col(LANES),
                                 col(8)]
    out_specs += [prow(w) for w in ROW_WIDTHS] + [srow(w) for w in ROW_WIDTHS]
    return pl.pallas_call(
        functools.partial(_inproj_kernel, npt=npt),
        grid=(nt,),
        in_specs=[row(d), full(g1), full(wmain), full(wuq), full(gq), full(gkv), full(aux),
                  pl.BlockSpec((tm, RT_COLS), tmap), pl.BlockSpec((tm, RT_COLS), tmap)],
        out_specs=out_specs,
        out_shape=out_shapes,
        compiler_params=_cparams(("arbitrary",)),
        name="inproj",
    )(x, g1, wmain, wuq, gq, gkv, aux, ctab, stab)


def _head_lane_mask(shape, h, width=HEAD_DIM):
    lane = lax.broadcasted_iota(I32, shape, 1)
    return (lane >= h * width) & (lane < (h + 1) * width)


def _split_heads_q(q):
    return [jnp.where(_head_lane_mask(q.shape, h), q, jnp.zeros_like(q)) for h in range(4)]


def _direct_attend(logits_list, sel, v_blk):
    out = None
    for h, logits in enumerate(logits_list):
        lm = jnp.where(sel, logits, NEG)
        m = jnp.max(lm, axis=1, keepdims=True)
        p = jnp.where(sel, jnp.exp2(logits - m), 0.0)
        l = jnp.sum(p, axis=1, keepdims=True)
        o = _dot(p.astype(BF16), v_blk) / l
        o = jnp.where(_head_lane_mask(o.shape, h), o, 0.0)
        out = o if out is None else out + o
    return out.astype(BF16)


def _lane_cumsum(x):
    n = x.shape[1]
    lane = lax.broadcasted_iota(I32, x.shape, 1)
    k = 1
    while k < n:
        x = x + jnp.where(lane >= k, pltpu.roll(x, k, 1), 0.0)
        k *= 2
    return x


def _key_to_float(key):
    bits = key ^ ((key >> 31) & 0x7FFFFFFF)
    return jnp.where(key < KEY_NEG_INF, -jnp.inf, pltpu.bitcast(bits, F32))


def _kth_largest(count_ge, shape, k):
    zero = jnp.zeros(shape, I32)
    c0 = count_ge(_key_to_float(zero))
    lo = jnp.where(c0 >= k, zero, jnp.full(shape, INT_MIN, I32))

    def bit_body(i, lo):
        trial = lo + jnp.left_shift(jnp.int32(1), 30 - i)
        c = count_ge(_key_to_float(trial))
        return jnp.where(c >= k, trial, lo)

    lo = lax.fori_loop(0, 31, bit_body, lo)
    return _key_to_float(lo)


def _kth_largest_multi(count_ge, shape, k, parts):
    zero = jnp.zeros(shape, I32)
    c0 = count_ge(tuple(_key_to_float(zero) for _ in range(parts)))
    lo = tuple(jnp.where(c >= k, zero, jnp.full(shape, INT_MIN, I32)) for c in c0)

    def bit_body(i, lo):
        step = jnp.left_shift(jnp.int32(1), 30 - i)
        trial = tuple(l + step for l in lo)
        c = count_ge(tuple(_key_to_float(t) for t in trial))
        return tuple(jnp.where(ci >= k, ti, li) for ci, ti, li in zip(c, trial, lo))

    lo = lax.fori_loop(0, 31, bit_body, lo)
    return tuple(_key_to_float(l) for l in lo)


def _tri128():
    r = lax.broadcasted_iota(I32, (LANES, LANES), 0)
    c = lax.broadcasted_iota(I32, (LANES, LANES), 1)
    return jnp.where(r <= c, 1.0, 0.0).astype(BF16)


def _tri(n):
    r = lax.broadcasted_iota(I32, (n, n), 0)
    c = lax.broadcasted_iota(I32, (n, n), 1)
    return jnp.where(c <= r, 1.0, 0.0).astype(BF16)


def _split_heads_qt(qt):
    row = lax.broadcasted_iota(I32, qt.shape, 0)
    return [jnp.where((row >= h * HEAD_DIM) & (row < (h + 1) * HEAD_DIM), qt, jnp.zeros_like(qt))
            for h in range(4)]


def _flash_t(last, tk, tq, logits_fn, sel_fn, vt_ref, acc_ref):
    acc_ref[...] = jnp.zeros(acc_ref.shape, F32)

    def body(kb, carry, diag):
        ms, ls = carry
        off = pl.multiple_of(kb * tk, tk)
        sel = sel_fn(off, diag)
        lgs = logits_fn(off)
        new_m, new_l = [], []
        for h in range(4):
            lm = lgs[h] if sel is None else jnp.where(sel, lgs[h], NEG)
            m_new = jnp.maximum(ms[h], jnp.max(lm, axis=0, keepdims=True))
            p = jnp.exp2(lm - m_new)
            alpha = jnp.exp2(ms[h] - m_new)
            new_l.append(alpha * ls[h] + jnp.sum(p, axis=0, keepdims=True))
            rows = slice(h * HEAD_DIM, (h + 1) * HEAD_DIM)
            acc_ref[rows, :] = alpha * acc_ref[rows, :] + _dot(vt_ref[rows, pl.ds(off, tk)], p.astype(BF16))
            new_m.append(m_new)
        return tuple(new_m), tuple(new_l)

    init = (tuple(jnp.full((1, tq), NEG / 2, F32) for _ in range(4)),
            tuple(jnp.zeros((1, tq), F32) for _ in range(4)))
    carry = lax.fori_loop(0, last, functools.partial(body, diag=False), init)
    _, ls = body(last, carry, True)
    out = jnp.concatenate([acc_ref[h * HEAD_DIM:(h + 1) * HEAD_DIM, :] / ls[h] for h in range(4)], axis=0)
    return out.T.astype(BF16)


def _tie_select(eq, rem, tri):
    parts = []
    for c in range(eq.shape[1] // LANES):
        e = eq[:, c * LANES:(c + 1) * LANES]
        pre = _dot(jnp.where(e, 1.0, 0.0).astype(BF16), tri)
        parts.append(jnp.where(e & (pre <= rem), 1, 0))
        rem = rem - pre[:, LANES - 1:LANES]
    return jnp.concatenate(parts, axis=1), rem


def _dsa_query_prep(iq, smq):
    lane = lax.broadcasted_iota(I32, iq.shape, 1)
    iqh = []
    for h in range(IDX_HEADS):
        r = iq if h == 0 else pltpu.roll(iq, LANES - IDX_DIM * h, 1)
        iqh.append(jnp.where(lane < IDX_DIM, r, 0.0).astype(BF16))
    wcol = [smq[:, SM_IW + h:SM_IW + h + 1] for h in range(IDX_HEADS)]
    return iqh, wcol


def _dsa_scores(iqh, wcol, ik_blk, contract32=False):
    sc = None
    for h in range(IDX_HEADS):
        a = iqh[h][:, :IDX_DIM] if contract32 else iqh[h]
        z = _dot_nt(a, ik_blk)
        t = jnp.maximum(z, 0.0) * wcol[h]
        sc = t if sc is None else sc + t
    return sc


def _dsa_prompt_kernel(qt_ref, iqt_ref, iwt_ref, k_ref, vt_ref, smk_ref, o_ref,
                       sc_ref, acc_ref, rem_ref, *, tq, tk, topk):
    j = pl.program_id(1)
    nkb = j + 1
    q0 = j * tq
    iqt = iqt_ref[...]
    pad = jnp.zeros((LANES - IDX_DIM, tq), BF16)
    iqh = [jnp.concatenate([iqt[IDX_DIM * h:IDX_DIM * (h + 1), :], pad], axis=0) for h in range(IDX_HEADS)]
    iw = iwt_ref[...]
    row = lax.broadcasted_iota(I32, (tk, tq), 0)
    col = lax.broadcasted_iota(I32, (tk, tq), 1)
    qchunk = (q0 + col) >> 6

    def visible(off):
        return ((off + row) >> 6) <= qchunk

    def score_body(kb, carry, diag):
        off = pl.multiple_of(kb * tk, tk)
        ik = smk_ref[pl.ds(off, tk), :].astype(BF16)
        sc = None
        for h in range(IDX_HEADS):
            t = jnp.maximum(_dot(ik, iqh[h]), 0.0) * iw[h:h + 1, :]
            sc = t if sc is None else sc + t
        sc_ref[pl.ds(off, tk), :] = jnp.where(visible(off), sc, -jnp.inf) if diag else sc
        return carry

    lax.fori_loop(0, j, functools.partial(score_body, diag=False), 0)
    score_body(j, 0, True)

    def count_cmp(trial, strict):
        def body(kb, c):
            off = pl.multiple_of(kb * tk, tk)
            s = sc_ref[pl.ds(off, tk), :]
            hit = (s > trial) if strict else (s >= trial)
            return c + jnp.sum(jnp.where(hit, 1, 0), axis=0, keepdims=True)
        return lax.fori_loop(0, nkb, body, jnp.zeros((1, tq), I32))

    thr = _kth_largest(lambda t: count_cmp(t, False), (1, tq), topk)
    rem_ref[0:1, :] = (topk - count_cmp(thr, True)).astype(F32)
    qh = _split_heads_qt(qt_ref[...])
    tri = _tri(tk)

    def sel_fn(off, diag):
        s = sc_ref[pl.ds(off, tk), :]
        eq = s == thr
        pre = _dot(tri, jnp.where(eq, 1.0, 0.0).astype(BF16))
        rem = rem_ref[0:1, :]
        keep = eq & (pre <= rem)
        rem_ref[0:1, :] = rem - pre[tk - 1:tk, :]
        sel = (s > thr) | keep
        return (visible(off) & sel) if diag else sel

    def logits_fn(off):
        k_blk = k_ref[pl.ds(off, tk), :]
        return [_dot(k_blk, qh[h]) for h in range(A_HEADS)]

    o_ref[...] = _flash_t(j, tk, tq, logits_fn, sel_fn, vt_ref, acc_ref)


def _dsa_prompt(aqt, iqt, iwt, akb, avt, sm, *, batch, seq, tq, topk, n_out):
    nq = seq // tq
    tk = tq
    qcol = lambda w: pl.BlockSpec((w, tq), lambda b, j: (0, b * nq + j))
    krow = lambda w: pl.BlockSpec((seq, w), lambda b, j: (b, 0))
    kern = functools.partial(_dsa_prompt_kernel, tq=tq, tk=tk, topk=topk)
    return pl.pallas_call(
        kern,
        grid=(batch, nq),
        in_specs=[qcol(HW), qcol(LANES), qcol(8), krow(HW),
                  pl.BlockSpec((HW, seq), lambda b, j: (0, b)), krow(LANES)],
        out_specs=pl.BlockSpec((tq, HW), lambda b, j: (b * nq + j, 0)),
        out_shape=jax.ShapeDtypeStruct((n_out, HW), BF16),
        scratch_shapes=[pltpu.VMEM((seq, tq), F32), pltpu.VMEM((HW, tq), F32), pltpu.VMEM((8, tq), F32)],
        compiler_params=_cparams(("parallel", "arbitrary")),
        name="dsa_prompt",
    )(aqt, iqt, iwt, akb, avt, sm)


def _dsa_sample_kernel(q_ref, iq_ref, smq_ref, kn_ref, vn_ref, ck_ref, cv_ref, cik_ref, prev_ref, o_ref,
                       *, tq, past, topk, nb):
    pad = LANES - tq
    lk = past + LANES
    lane = lax.broadcasted_iota(I32, (tq, LANES), 1)
    col = lax.broadcasted_iota(I32, (tq, lk), 1)
    vis = col < past + tq
    scs = []
    for s in range(nb):
        rows = slice(s * tq, (s + 1) * tq)
        smq = smq_ref[rows, :]
        iqh, wcol = _dsa_query_prep(iq_ref[rows, :], smq)
        sc_c = _dsa_scores(iqh, wcol, cik_ref[s].astype(BF16), contract32=True)
        ik_new = jnp.concatenate([smq.astype(BF16), jnp.zeros((pad, LANES), BF16)], axis=0)
        sc_n = _dsa_scores(iqh, wcol, ik_new)
        sc_n = jnp.where(lane < tq, sc_n, -jnp.inf)
        scs.append(jnp.concatenate([sc_c, sc_n], axis=1))

    def count_ge(trials):
        return tuple(jnp.sum(jnp.where(sc >= t, 1, 0), axis=1, keepdims=True) for sc, t in zip(scs, trials))

    thr_all = _kth_largest_multi(count_ge, (tq, 1), topk, nb)
    tri = _tri128()
    for s in range(nb):
        rows = slice(s * tq, (s + 1) * tq)
        sc = scs[s]
        thr = thr_all[s]
        n_gt = jnp.sum(jnp.where(sc > thr, 1, 0), axis=1, keepdims=True)
        need = (topk - n_gt).astype(F32)
        tie, _ = _tie_select(sc == thr, need, tri)
        sel = vis & ((sc > thr) | (tie > 0))
        k_all = jnp.concatenate([ck_ref[s].astype(BF16), kn_ref[rows, :], jnp.zeros((pad, HW), BF16)], axis=0)
        v_all = jnp.concatenate([cv_ref[s].astype(BF16), vn_ref[rows, :], jnp.zeros((pad, HW), BF16)], axis=0)
        qh = _split_heads_q(q_ref[rows, :])
        o_ref[rows, :] = _direct_attend([_dot_nt(qh[h], k_all) for h in range(A_HEADS)], sel, v_all)


def _dsa_sample(aq, iq, sm, akb, avb, cache_k, cache_v, cache_ik, o_prev, *, n_prompt, dbatch, dseq, topk):
    past = cache_k.shape[1]
    nb = 4 if dbatch % 4 == 0 else 1
    base = n_prompt // (nb * dseq)
    qrow = lambda w: pl.BlockSpec((nb * dseq, w), lambda b: (base + b, 0))
    crow = lambda w: pl.BlockSpec((nb, past, w), lambda b: (b, 0, 0))
    kern = functools.partial(_dsa_sample_kernel, tq=dseq, past=past, topk=topk, nb=nb)
    return pl.pallas_call(
        kern,
        grid=(dbatch // nb,),
        in_specs=[qrow(HW), qrow(LANES), qrow(LANES), qrow(HW), qrow(HW),
                  crow(HW), crow(HW), crow(IDX_DIM), pl.BlockSpec(memory_space=pl.ANY)],
        out_specs=qrow(HW),
        out_shape=jax.ShapeDtypeStruct(o_prev.shape, BF16),
        input_output_aliases={8: 0},
        compiler_params=_cparams(("parallel",)),
        name="dsa_sample",
    )(aq, iq, sm, akb, avb, cache_k, cache_v, cache_ik, o_prev)


def _split3(x):
    hi = x.astype(BF16)
    r1 = x - hi.astype(F32)
    mid = r1.astype(BF16)
    lo = (r1 - mid.astype(F32)).astype(BF16)
    return hi, mid, lo


def _fox_prompt_kernel(qt_ref, k_ref, vt_ref, smk_ref, o_ref, ck3_ref, acc_ref, *, tq, tk, seq):
    j = pl.program_id(1)
    tri = _tri(tk)

    @pl.when(j == 0)
    def _():
        lane = lax.broadcasted_iota(I32, (tk, LANES), 1)
        is_f = (lane >= SM_BF) & (lane < SM_BF + B_HEADS)
        carry = jnp.zeros((1, LANES), F32)
        for r in range(seq // tk):
            lf = jnp.where(is_f, smk_ref[r * tk:(r + 1) * tk, :], 0.0)
            cs = carry
            for piece in _split3(lf):
                cs = cs + _dot(tri, piece)
            carry = cs[tk - 1:tk, :]
            nb = cs * (-LOG2E)
            hi = nb.astype(BF16).astype(F32)
            r1 = nb - hi
            mid = r1.astype(BF16).astype(F32)
            lo = r1 - mid
            slab = jnp.where(is_f, hi, pltpu.roll(mid, B_HEADS, 1) + pltpu.roll(lo, 2 * B_HEADS, 1))
            ck3_ref[r * tk:(r + 1) * tk, :] = slab.astype(BF16)

    q0 = j * tq
    row = lax.broadcasted_iota(I32, (tk, tq), 0)
    col = lax.broadcasted_iota(I32, (tk, tq), 1)
    qpos = q0 + col
    qh = _split_heads_qt(qt_ref[...])
    srow = lax.broadcasted_iota(I32, (LANES, tq), 0)
    pick = [jnp.where((srow == SM_BF + h) | (srow == SM_BF + B_HEADS + h) | (srow == SM_BF + 2 * B_HEADS + h),
                      1.0, 0.0).astype(BF16) for h in range(B_HEADS)]
    qx = [jnp.concatenate([qh[h], pick[h]], axis=0) for h in range(B_HEADS)]

    def sel_fn(off, diag):
        return ((off + row) <= qpos) if diag else None

    def logits_fn(off):
        kx = jnp.concatenate([k_ref[pl.ds(off, tk), :], ck3_ref[pl.ds(off, tk), :]], axis=1)
        return [_dot(kx, qx[h]) for h in range(B_HEADS)]

    o_ref[...] = _flash_t(j, tk, tq, logits_fn, sel_fn, vt_ref, acc_ref)


def _fox_prompt(bqt, bkb, bvt, sm, *, batch, seq, tq, n_out):
    nq = seq // tq
    tk = tq
    krow = lambda w: pl.BlockSpec((seq, w), lambda b, j: (b, 0))
    kern = functools.partial(_fox_prompt_kernel, tq=tq, tk=tk, seq=seq)
    return pl.pallas_call(
        kern,
        grid=(batch, nq),
        in_specs=[pl.BlockSpec((HW, tq), lambda b, j: (0, b * nq + j)), krow(HW),
                  pl.BlockSpec((HW, seq), lambda b, j: (0, b)), krow(LANES)],
        out_specs=pl.BlockSpec((tq, HW), lambda b, j: (b * nq + j, 0)),
        out_shape=jax.ShapeDtypeStruct((n_out, HW), BF16),
        scratch_shapes=[pltpu.VMEM((seq, LANES), BF16), pltpu.VMEM((HW, tq), F32)],
        compiler_params=_cparams(("parallel", "arbitrary")),
        name="fox_prompt",
    )(bqt, bkb, bvt, sm)


def _fox_sample_kernel(q_ref, kn_ref, vn_ref, ck_ref, cv_ref, clf_ref, nlf_ref, prev_ref, o_ref, *, tq, past):
    pad = LANES - tq
    cum_c = _lane_cumsum(clf_ref[0])
    cum_n = _lane_cumsum(nlf_ref[0])
    bias_c = (cum_c[:, past - 1:past] - cum_c) * LOG2E
    bias_n = cum_n * (-LOG2E)
    bias = jnp.concatenate([bias_c, bias_n], axis=1)
    lk = past + LANES
    row = lax.broadcasted_iota(I32, (tq, lk), 0)
    col = lax.broadcasted_iota(I32, (tq, lk), 1)
    sel = col <= past + row
    k_all = jnp.concatenate([ck_ref[0].astype(BF16), kn_ref[...], jnp.zeros((pad, HW), BF16)], axis=0)
    v_all = jnp.concatenate([cv_ref[0].astype(BF16), vn_ref[...], jnp.zeros((pad, HW), BF16)], axis=0)
    qh = _split_heads_q(q_ref[...])
    logits = [_dot_nt(qh[h], k_all) + bias[h:h + 1, :] for h in range(B_HEADS)]
    o_ref[...] = _direct_attend(logits, sel, v_all)


def _fox_sample(bq, bkb, bvb, cache_k, cache_v, cache_lft, new_lft, o_prev, *, n_prompt, dbatch, dseq):
    past = cache_k.shape[1]
    base = n_prompt // dseq
    qrow = lambda w: pl.BlockSpec((dseq, w), lambda b: (base + b, 0))
    crow = lambda w: pl.BlockSpec((1, past, w), lambda b: (b, 0, 0))
    kern = functools.partial(_fox_sample_kernel, tq=dseq, past=past)
    return pl.pallas_call(
        kern,
        grid=(dbatch,),
        in_specs=[qrow(HW), qrow(HW), qrow(HW), crow(HW), crow(HW),
                  pl.BlockSpec((1, 8, past), lambda b: (b, 0, 0)),
                  pl.BlockSpec((1, 8, LANES), lambda b: (b, 0, 0)),
                  pl.BlockSpec(memory_space=pl.ANY)],
        out_specs=qrow(HW),
        out_shape=jax.ShapeDtypeStruct(o_prev.shape, BF16),
        input_output_aliases={7: 0},
        compiler_params=_cparams(("parallel",)),
        name="fox_sample",
    )(bq, bkb, bvb, cache_k, cache_v, cache_lft, new_lft, o_prev)


def _mla_logits(q, kfull_blk):
    return [_dot_nt(q[:, h * LANES:(h + 1) * LANES], kfull_blk[:, h * LANES:(h + 1) * LANES])
            for h in range(C_HEADS)]


def _mla_prompt_kernel(qt_ref, ckv_ref, ckvt_ref, smk_ref, w1_ref, w2_ref, w1vt_ref, o_ref,
                       kf_ref, vt_ref, acc_ref, *, tq, tk, seq):
    j = pl.program_id(1)

    @pl.when(j == 0)
    def _():
        rc = 512 if seq % 512 == 0 else tk
        for r in range(seq // rc):
            kf = (_dot(ckv_ref[r * rc:(r + 1) * rc, :].astype(BF16), w1_ref[:, 0:512])
                  + _dot(smk_ref[r * rc:(r + 1) * rc, :].astype(BF16), w2_ref[:, 0:512]))
            kf_ref[r * rc:(r + 1) * rc, :] = kf.astype(BF16)
            vt_ref[:, r * rc:(r + 1) * rc] = _dot(w1vt_ref[...], ckvt_ref[:, r * rc:(r + 1) * rc]).astype(BF16)

    q0 = j * tq
    row = lax.broadcasted_iota(I32, (tk, tq), 0)
    col = lax.broadcasted_iota(I32, (tk, tq), 1)
    qchunk = (q0 + col) >> 6
    qt = qt_ref[...]

    def sel_fn(off, diag):
        return (((off + row) >> 6) <= qchunk) if diag else None

    def logits_fn(off):
        kf = kf_ref[pl.ds(off, tk), :]
        return [_dot(kf[:, h * LANES:(h + 1) * LANES], qt[h * LANES:(h + 1) * LANES, :])
                for h in range(C_HEADS)]

    o_ref[...] = _flash_t(j, tk, tq, logits_fn, sel_fn, vt_ref, acc_ref)


def _mla_prompt(qct, ckv, ckvt, sm, w1, w2, w1vt, *, batch, seq, tq, n_out):
    nq = seq // tq
    tk = tq
    krow = lambda w: pl.BlockSpec((seq, w), lambda b, j: (b, 0))
    full = lambda a: pl.BlockSpec(a.shape, lambda b, j: (0,) * a.ndim)
    kern = functools.partial(_mla_prompt_kernel, tq=tq, tk=tk, seq=seq)
    return pl.pallas_call(
        kern,
        grid=(batch, nq),
        in_specs=[pl.BlockSpec((512, tq), lambda b, j: (0, b * nq + j)), krow(LANES),
                  pl.BlockSpec((LANES, seq), lambda b, j: (0, b)), krow(LANES),
                  full(w1), full(w2), full(w1vt)],
        out_specs=pl.BlockSpec((tq, HW), lambda b, j: (b * nq + j, 0)),
        out_shape=jax.ShapeDtypeStruct((n_out, HW), BF16),
        scratch_shapes=[pltpu.VMEM((seq, 512), BF16), pltpu.VMEM((HW, seq), BF16),
                        pltpu.VMEM((HW, tq), F32)],
        compiler_params=_cparams(("parallel", "arbitrary")),
        name="mla_prompt",
    )(qct, ckv, ckvt, sm, w1, w2, w1vt)


def _mla_sample_kernel(q_ref, ckvn_ref, smn_ref, cckv_ref, ckr_ref, w1_ref, w2_ref, w2c_ref, prev_ref, o_ref,
                       *, tq, past):
    pad = LANES - tq
    kv_c = _dot(cckv_ref[0].astype(BF16), w1_ref[...]) + _dot(ckr_ref[0].astype(BF16), w2c_ref[...])
    kv_n = _dot(ckvn_ref[...].astype(BF16), w1_ref[...]) + _dot(smn_ref[...].astype(BF16), w2_ref[...])
    kv = jnp.concatenate([kv_c, kv_n, jnp.zeros((pad, 768), F32)], axis=0).astype(BF16)
    lk = past + LANES
    col = lax.broadcasted_iota(I32, (tq, lk), 1)
    sel = col < past + tq
    o_ref[...] = _direct_attend(_mla_logits(q_ref[...], kv[:, :512]), sel, kv[:, 512:])


def _mla_sample(qc, ckv, sm, cache_ckv, cache_kr, w1, w2, w2c, o_prev, *, n_prompt, dbatch, dseq):
    past = cache_ckv.shape[1]
    base = n_prompt // dseq
    qrow = lambda w: pl.BlockSpec((dseq, w), lambda b: (base + b, 0))
    crow = lambda w: pl.BlockSpec((1, past, w), lambda b: (b, 0, 0))
    full = lambda a: pl.BlockSpec(a.shape, lambda b: (0,) * a.ndim)
    kern = functools.partial(_mla_sample_kernel, tq=dseq, past=past)
    return pl.pallas_call(
        kern,
        grid=(dbatch,),
        in_specs=[qrow(512), qrow(LANES), qrow(LANES), crow(C_KV_RANK), crow(C_ROPE),
                  full(w1), full(w2), full(w2c), pl.BlockSpec(memory_space=pl.ANY)],
        out_specs=qrow(HW),
        out_shape=jax.ShapeDtypeStruct(o_prev.shape, BF16),
        input_output_aliases={8: 0},
        compiler_params=_cparams(("parallel",)),
        name="mla_sample",
    )(qc, ckv, sm, cache_ckv, cache_kr, w1, w2, w2c, o_prev)


def _merge_kernel(x_ref, g1_ref, oa_ref, ob_ref, oc_ref, wg_ref, wua_ref, wub_ref, wuc_ref, wout_ref,
                  xo_ref):
    x = x_ref[...]
    d = x.shape[1]
    h = _rmsnorm(x, g1_ref[...]).astype(BF16)
    gates = jax.nn.sigmoid(_dot(h, wg_ref[...]))
    merged = (gates[:, 0:d] * _dot(oa_ref[...], wua_ref[...])
              + gates[:, d:2 * d] * _dot(ob_ref[...], wub_ref[...])
              + gates[:, 2 * d:3 * d] * _dot(oc_ref[...], wuc_ref[...]))
    xo_ref[...] = x + _dot(merged.astype(BF16), wout_ref[...])


def _merge(x, g1, oa, ob, oc, wg, wua, wub, wuc, wout, *, tm):
    n, d = x.shape
    row = lambda w: pl.BlockSpec((tm, w), lambda i: (i, 0))
    full = lambda a: pl.BlockSpec(a.shape, lambda i: (0,) * a.ndim)
    return pl.pallas_call(
        _merge_kernel,
        grid=(n // tm,),
        in_specs=[row(d), full(g1), row(HW), row(HW), row(HW),
                  full(wg), full(wua), full(wub), full(wuc), full(wout)],
        out_specs=row(d),
        out_shape=jax.ShapeDtypeStruct((n, d), F32),
        compiler_params=_cparams(("parallel",)),
        name="merge",
    )(x, g1, oa, ob, oc, wg, wua, wub, wuc, wout)


WORD = I32


def _row_words(x):
    u = pltpu.bitcast(x.astype(BF16).astype(F32), I32)
    return u | lax.shift_right_logical(u, 16)


def _bcast_rows_bf16(words_row, rows):
    return pltpu.bitcast(jnp.broadcast_to(words_row, (rows // 2, words_row.shape[1])), BF16)


def _top_values(s, k, with_rank=False):
    vals = []
    rank = jnp.full(s.shape, float(k), F32)
    for r in range(k):
        m = jnp.max(s, axis=0, keepdims=True)
        vals.append(m)
        hit = s == m
        if with_rank:
            rank = jnp.where(hit, float(r), rank)
        s = jnp.where(hit, -jnp.inf, s)
    return (vals, rank) if with_rank else vals


def _peer_kernel(x_ref, g2_ref, wqt_ref, k1_ref, k2_ref, u_ref, vt_ref, o_ref,
                 ht_ref, qt_ref, acc_ref, w_ref, rank_ref, e2_ref, cnt_ref, c_ref, v2_ref, cand_ref,
                 *, nk, na):
    k = pl.program_id(1)
    tm = x_ref.shape[0]
    half = PEER_QDIM // 2

    @pl.when(k == 0)
    def _():
        h = _rmsnorm(x_ref[...], g2_ref[...])
        ht = h.T.astype(BF16)
        ht_ref[...] = ht
        qt_ref[...] = _dot(wqt_ref[...], ht).astype(BF16)
        acc_ref[...] = jnp.zeros(acc_ref.shape, F32)

        def select_body(hh, carry):
            qrow = pl.multiple_of(hh * PEER_QDIM, PEER_QDIM)
            krow = pl.multiple_of(hh * nk, nk)
            q1 = qt_ref[pl.ds(qrow, half), :]
            q2 = qt_ref[pl.ds(qrow + half, half), :]
            s1 = _dot(k1_ref[pl.ds(krow, nk), :], q1)
            s2 = _dot(k2_ref[pl.ds(krow, nk), :], q2)
            v1 = _top_values(s1, PEER_TOPK)
            v2, rank2 = _top_values(s2, PEER_TOPK, with_rank=True)
            for jj in range(PEER_TOPK):
                v2_ref[jj:jj + 1, :] = v2[jj]
            cand_ref[...] = jnp.full(cand_ref.shape, -jnp.inf, F32)
            off = 0
            for i in range(PEER_TOPK):
                nj = PEER_TOPK // (i + 1)
                cand_ref[off:off + nj, :] = v1[i] + v2_ref[0:nj, :]
                off += nj
            cand = cand_ref[...]
            tau = _top_values(cand, PEER_TOPK)[-1]
            top = v1[0] + v2[0]
            z = jnp.sum(jnp.where(cand >= tau, jnp.exp(cand - top), 0.0), axis=0, keepdims=True)
            cnt = jnp.zeros((nk, tm), F32)
            for jj in range(PEER_TOPK):
                cnt = jnp.where((s1 + v2[jj]) >= tau, float(jj + 1), cnt)
            rank_ref[hh] = rank2.astype(BF16)
            e2_ref[hh] = jnp.exp(s2 - v2[0]).astype(BF16)
            cnt_ref[hh] = _row_words(cnt)
            c_ref[hh] = _row_words(jnp.exp(s1 - v1[0]) / z)
            return carry

        lax.fori_loop(0, PEER_HEADS, select_body, 0)

    a0 = pl.multiple_of(k * na, na)
    cnt_rows = [cnt_ref[hh, pl.ds(a0, na), :] for hh in range(PEER_HEADS)]
    c_rows = [c_ref[hh, pl.ds(a0, na), :] for hh in range(PEER_HEADS)]
    zero = jnp.zeros((nk, tm), BF16)
    for ai in range(na):
        w = None
        for hh in range(PEER_HEADS):
            cb = _bcast_rows_bf16(cnt_rows[hh][ai:ai + 1, :], nk)
            cc = _bcast_rows_bf16(c_rows[hh][ai:ai + 1, :], nk)
            t = jnp.where(rank_ref[hh] < cb, e2_ref[hh], zero) * cc
            w = t if w is None else w + t
        w_ref[ai * nk:(ai + 1) * nk, :] = w
    act = jax.nn.gelu(_dot(u_ref[...], ht_ref[...]).astype(BF16))
    acc_ref[...] += _dot(vt_ref[...], w_ref[...] * act)

    @pl.when(k == pl.num_programs(1) - 1)
    def _():
        o_ref[...] = x_ref[...] + acc_ref[...].T


def _peer(x, g2, wqt, k1, k2, u, vt, *, tm, na):
    n, d = x.shape
    nk = k1.shape[0] // PEER_HEADS
    te = na * nk
    nchunks = nk // na
    ncand = sum(PEER_TOPK // (i + 1) for i in range(PEER_TOPK))
    ncand = -(-ncand // 8) * 8
    full = lambda a: pl.BlockSpec(a.shape, lambda i, k: (0,) * a.ndim)
    kern = functools.partial(_peer_kernel, nk=nk, na=na)
    return pl.pallas_call(
        kern,
        grid=(n // tm, nchunks),
        in_specs=[pl.BlockSpec((tm, d), lambda i, k: (i, 0)), full(g2), full(wqt), full(k1), full(k2),
                  pl.BlockSpec((te, d), lambda i, k: (k, 0)),
                  pl.BlockSpec((None, d, te), lambda i, k: (k, 0, 0))],
        out_specs=pl.BlockSpec((tm, d), lambda i, k: (i, 0)),
        out_shape=jax.ShapeDtypeStruct((n, d), F32),
        scratch_shapes=[pltpu.VMEM((d, tm), BF16), pltpu.VMEM((PEER_HEADS * PEER_QDIM, tm), BF16),
                        pltpu.VMEM((d, tm), F32), pltpu.VMEM((te, tm), BF16),
                        pltpu.VMEM((PEER_HEADS, nk, tm), BF16), pltpu.VMEM((PEER_HEADS, nk, tm), BF16),
                        pltpu.VMEM((PEER_HEADS, nk, tm), WORD), pltpu.VMEM((PEER_HEADS, nk, tm), WORD),
                        pltpu.VMEM((PEER_TOPK, tm), F32), pltpu.VMEM((ncand, tm), F32)],
        compiler_params=_cparams(("parallel", "arbitrary")),
        name="peer",
    )(x, g2, wqt, k1, k2, u, vt)


def _final_norm_kernel(x_ref, g_ref, op_ref, os_ref, *, npt):
    y = _rmsnorm(x_ref[...], g_ref[...])
    i = pl.program_id(0)

    @pl.when(i < npt)
    def _():
        op_ref[...] = y

    @pl.when(i >= npt)
    def _():
        os_ref[...] = y


def _final_norm(x, g, *, tm, n_prompt):
    n, d = x.shape
    npt = n_prompt // tm
    return pl.pallas_call(
        functools.partial(_final_norm_kernel, npt=npt),
        grid=(n // tm,),
        in_specs=[pl.BlockSpec((tm, d), lambda i: (i, 0)), pl.BlockSpec((1, d), lambda i: (0, 0))],
        out_specs=[pl.BlockSpec((tm, d), lambda i: (jnp.minimum(i, npt - 1), 0)),
                   pl.BlockSpec((tm, d), lambda i: (jnp.maximum(i - npt, 0), 0))],
        out_shape=[jax.ShapeDtypeStruct((n_prompt, d), F32), jax.ShapeDtypeStruct((n - n_prompt, d), F32)],
        compiler_params=_cparams(("arbitrary",)),
        name="final_norm",
    )(x, g)


def _rope_tables(pos):
    rows = pos.shape[0]

    def cs(half):
        inv = jnp.power(ROPE_THETA, -jnp.arange(half, dtype=F32) / half)
        ang = pos.astype(F32)[:, None] * inv[None, :]
        c, s = jnp.cos(ang), jnp.sin(ang)
        return jnp.concatenate([c, c], axis=1), jnp.concatenate([-s, s], axis=1)

    c64, s64 = cs(HEAD_DIM // 2)
    c32, s32 = cs(IDX_DIM // 2)
    one32 = jnp.ones((rows, 32), F32)
    zero32 = jnp.zeros((rows, 32), F32)
    c_sm = jnp.concatenate([c32, c32, one32, one32], axis=1)
    s_sm = jnp.concatenate([s32, s32, zero32, zero32], axis=1)
    c_qc = jnp.concatenate([one32, one32, c32, one32], axis=1)
    s_qc = jnp.concatenate([zero32, zero32, s32, zero32], axis=1)
    ctab = jnp.concatenate([jnp.tile(c64, (1, 4)), jnp.tile(c32, (1, 4)), c_sm, jnp.tile(c_qc, (1, 4))], axis=1)
    stab = jnp.concatenate([jnp.tile(s64, (1, 4)), jnp.tile(s32, (1, 4)), s_sm, jnp.tile(s_qc, (1, 4))], axis=1)
    return ctab, stab


def _split_w_in(w_in):
    sizes = (HW, HW, HW, IDX_HEADS * IDX_DIM, IDX_DIM, IDX_HEADS, HW, HW, HW, B_HEADS,
             C_Q_RANK, C_KV_RANK, C_ROPE)
    offs = np.cumsum((0,) + sizes)
    parts = [w_in[..., int(offs[i]):int(offs[i + 1])] for i in range(len(sizes))]
    gates = w_in[..., int(offs[-1]):]
    return parts, gates


def _pack_weights(w_in, w_uq, w_ukv, b_forget):
    depth, d, _ = w_in.shape
    (a_q, a_k, a_v, i_q, i_k, i_w, b_q, b_k, b_v, b_f, c_q, c_kv, c_kr), gates = _split_w_in(w_in)
    z = lambda w: jnp.zeros((depth, d, w), F32)
    small = jnp.concatenate([i_k, c_kr, b_f, z(4), i_w, z(LANES - SM_IW - IDX_HEADS)], axis=-1)
    wmain = jnp.concatenate([a_q, a_k, a_v, b_q, b_k, b_v, c_q, c_kv, i_q, small], axis=-1).astype(BF16)
    wg = gates.astype(BF16)

    uq = w_uq.reshape(depth, C_Q_RANK, C_HEADS, C_NOPE + C_ROPE)
    uq = jnp.concatenate([uq, jnp.zeros((depth, C_Q_RANK, C_HEADS, 32), F32)], axis=-1)
    wuq = uq.reshape(depth, C_Q_RANK, C_HEADS * LANES).astype(BF16)

    ukv = w_ukv.reshape(depth, C_KV_RANK, C_HEADS, C_NOPE + C_VDIM)
    kn = jnp.concatenate([ukv[..., :C_NOPE], jnp.zeros((depth, C_KV_RANK, C_HEADS, 64), F32)], axis=-1)
    w1 = jnp.concatenate([kn.reshape(depth, C_KV_RANK, 512),
                          ukv[..., C_NOPE:].reshape(depth, C_KV_RANK, HW)], axis=-1).astype(BF16)
    w1vt = jnp.swapaxes(w1[:, :, 512:], 1, 2)
    eye = np.zeros((C_ROPE, 768), np.float32)
    for h in range(C_HEADS):
        for r in range(C_ROPE):
            eye[r, h * LANES + C_NOPE + r] = 1.0
    w2c = jnp.asarray(eye).astype(BF16)
    w2_np = np.zeros((LANES, 768), np.float32)
    w2_np[SM_KR:SM_KR + C_ROPE] = eye
    w2 = jnp.asarray(w2_np).astype(BF16)

    aux = jnp.zeros((depth, 8, LANES), F32)
    aux = aux.at[:, 0, SM_BF:SM_BF + B_HEADS].set(b_forget)
    scale = np.ones((LANES,), np.float32)
    scale[SM_IW:SM_IW + IDX_HEADS] = (IDX_HEADS * IDX_DIM) ** -0.5
    aux = aux.at[:, 1, :].set(jnp.asarray(scale))
    return wmain, wg, wuq, w1, w1vt, w2, w2c, aux


def _tile_rows(n):
    for t in (512, 256, 128):
        if n % t == 0:
            return t
    raise ValueError(f"token count {n} is not a multiple of 128")


def kernel(x_prompt, x_sample, cache_k_a, cache_v_a, cache_kidx_a, cache_k_b, cache_v_b, cache_logf_b,
           cache_ckv_c, cache_krope_c, norm1_g, w_in, c_q_norm_g, c_kv_norm_g, w_uq, w_ukv, b_forget,
           w_up_a, w_up_b, w_up_c, w_out, norm2_g, peer_w_q, peer_keys, peer_u, peer_v, final_norm_g):
    batch, seq, d = x_prompt.shape
    dbatch, dseq, _ = x_sample.shape
    depth = w_in.shape[0]
    past = cache_k_a.shape[2]
    n_p = batch * seq
    n_s = dbatch * dseq
    n = n_p + n_s
    nk = peer_keys.shape[3]
    assert dseq == CHUNK and past % LANES == 0 and seq % 256 == 0

    tm = min(_tile_rows(seq), _tile_rows(n_s))
    tq = 512 if seq % 512 == 0 else 256
    topk_p = min(TOPK_MAX, seq // 4)
    topk_s = min(TOPK_MAX, (past + dseq) // 4)

    pos = jnp.concatenate([jnp.arange(seq, dtype=jnp.int32),
                           past + (jnp.arange(tm, dtype=jnp.int32) % dseq)])
    ctab, stab = _rope_tables(pos)

    wmain, wg, wuq, w1, w1vt, w2, w2c, aux = _pack_weights(w_in, w_uq, w_ukv, b_forget)
    wua, wub, wuc, wout = (w.astype(BF16) for w in (w_up_a, w_up_b, w_up_c, w_out))
    wqt = jnp.swapaxes(peer_w_q, 1, 2).astype(BF16)
    half = PEER_QDIM // 2
    k1 = peer_keys[:, :, 0].reshape(depth, PEER_HEADS * nk, half).astype(BF16)
    k2 = peer_keys[:, :, 1].reshape(depth, PEER_HEADS * nk, half).astype(BF16)
    pu = peer_u.astype(BF16)
    na = 16 if nk % 16 == 0 else nk
    pvt = peer_v.astype(BF16).reshape(depth, nk // na, na * nk, d).transpose(0, 1, 3, 2)
    clf_t = jnp.swapaxes(cache_logf_b, 2, 3)
    clf_t = jnp.concatenate([clf_t, jnp.zeros_like(clf_t)], axis=2)

    g_row = lambda g: g.reshape(1, -1)
    x = jnp.concatenate([x_prompt.reshape(n_p, d), x_sample.reshape(n_s, d)], axis=0)
    nr = len(ROW_WIDTHS)
    rows_p = [[] for _ in range(nr)]
    rows_s = [[] for _ in range(nr)]

    for l in range(depth):
        g1 = g_row(norm1_g[l])
        outs = _inproj(x, g1, wmain[l], wuq[l], g_row(c_q_norm_g[l]), g_row(c_kv_norm_g[l]), aux[l],
                       ctab, stab, tm=tm, n_prompt=n_p, seq=seq)
        (aq, akb, avb, bq, bkb, bvb, qc, ckv, iq, sm, lft,
         aqt, avt, bqt, bvt, qct, iqt, ckvt, iwt) = outs[:19]
        for i in range(nr):
            rows_p[i].append(outs[19 + i])
            rows_s[i].append(outs[19 + nr + i])

        new_lft = lft[:, n_p:].reshape(8, dbatch, dseq).transpose(1, 0, 2)
        new_lft = jnp.concatenate([new_lft, jnp.zeros((dbatch, 8, LANES - dseq), F32)], axis=2)

        oa = _dsa_prompt(aqt, iqt, iwt, akb, avt, sm, batch=batch, seq=seq, tq=tq, topk=topk_p, n_out=n)
        oa = _dsa_sample(aq, iq, sm, akb, avb,
                         cache_k_a[l].reshape(dbatch, past, HW), cache_v_a[l].reshape(dbatch, past, HW),
                         cache_kidx_a[l], oa, n_prompt=n_p, dbatch=dbatch, dseq=dseq, topk=topk_s)
        ob = _fox_prompt(bqt, bkb, bvt, sm, batch=batch, seq=seq, tq=tq, n_out=n)
        ob = _fox_sample(bq, bkb, bvb,
                         cache_k_b[l].reshape(dbatch, past, HW), cache_v_b[l].reshape(dbatch, past, HW),
                         clf_t[l], new_lft, ob, n_prompt=n_p, dbatch=dbatch, dseq=dseq)
        oc = _mla_prompt(qct, ckv, ckvt, sm, w1[l], w2, w1vt[l], batch=batch, seq=seq, tq=tq, n_out=n)
        oc = _mla_sample(qc, ckv, sm, cache_ckv_c[l], cache_krope_c[l], w1[l], w2, w2c, oc,
                         n_prompt=n_p, dbatch=dbatch, dseq=dseq)

        x = _merge(x, g1, oa, ob, oc, wg[l], wua[l], wub[l], wuc[l], wout[l], tm=tm)
        x = _peer(x, g_row(norm2_g[l]), wqt[l], k1[l], k2[l], pu[l], pvt[l], tm=tm, na=na)

    y_p, y_s = _final_norm(x, g_row(final_norm_g), tm=tm, n_prompt=n_p)
    y_p = y_p.reshape(batch, seq, d)
    y_s = y_s.reshape(dbatch, dseq, d)

    tails = [(A_HEADS, HEAD_DIM), (A_HEADS, HEAD_DIM), (IDX_DIM,), (B_HEADS, HEAD_DIM), (B_HEADS, HEAD_DIM),
             (B_HEADS,), (C_KV_RANK,), (C_ROPE,)]
    outs_p = [jnp.stack(r).reshape((depth, batch, seq) + t) for r, t in zip(rows_p, tails)]
    outs_s = [jnp.stack(r).reshape((depth, dbatch, dseq) + t) for r, t in zip(rows_s, tails)]
    return (y_p, y_s, *outs_p, *outs_s)
```
